```python
import jax
import jax.numpy as jnp
from jax import lax
import numpy as np

D_MODEL = 1024
BATCH = 32
SEQ = 256
DEPTH = 2
DEC_BATCH = 4
DEC_SEQ = 4096
PAST_LEN = 512

GRID_W = 64
EPS = 1e-6
D_CONV = 512
CONV_K = 31
NA_HEADS = 8
HEAD_DIM = 64
D_ATT = NA_HEADS * HEAD_DIM
NA_ROWS = 8
NA_COLS = 16
NA_QCB = 16
NA_KCB = NA_QCB + NA_COLS
ATT_SCALE = HEAD_DIM ** -0.5
Q_BLOCK = 128
D_IN_EVEN = 3 * D_CONV + 4 * D_ATT
SPLITS_EVEN = (D_CONV, 2 * D_CONV, 3 * D_CONV, 3 * D_CONV + D_ATT, 3 * D_CONV + 2 * D_ATT, 3 * D_CONV + 3 * D_ATT)
D_POOL = D_MODEL
POOL_WINDOWS = (2, 4, 8, 16)
POOL_GROUPS = 4
POOL_GD = D_POOL // POOL_GROUPS

kernel_name = 'hybrid_conv_natten_pool_diffusion_step'


def rms_norm(x, g):
    xf = x.astype(jnp.float32)
    y = xf * lax.rsqrt(jnp.mean(xf * xf, axis=-1, keepdims=True) + EPS)
    return (y * g.astype(jnp.float32)).astype(x.dtype)


def layer_norm(x, g, b):
    xf = x.astype(jnp.float32)
    mu = jnp.mean(xf, axis=-1, keepdims=True)
    var = jnp.mean(jnp.square(xf - mu), axis=-1, keepdims=True)
    y = (xf - mu) * lax.rsqrt(var + EPS)
    return (y * g.astype(jnp.float32) + b.astype(jnp.float32)).astype(x.dtype)


def ada_modulate(x, cond, norm_g, w_ada, b_ada):
    m = jax.nn.silu(cond) @ w_ada + b_ada
    shift, scale, gate = jnp.split(m, 3, axis=-1)
    h = rms_norm(x, norm_g) * (1.0 + scale[:, None, :]) + shift[:, None, :]
    return h, gate[:, None, :]


def conformer_conv(a, b, conv_w, conv_b, ln_g, ln_b):
    u = a * jax.nn.sigmoid(b)
    y = lax.conv_general_dilated(u, conv_w[:, None, :], window_strides=(1,),
                                 padding=((CONV_K // 2, CONV_K // 2),),
                                 dimension_numbers=('NWC', 'WIO', 'NWC'),
                                 feature_group_count=D_CONV)
    y = layer_norm(y + conv_b, ln_g, ln_b)
    return jax.nn.silu(y)


def even_projections(h, w_in, q_norm, k_norm):
    bsz, L, _ = h.shape
    a, b, ga, q, k, v, gb = jnp.split(h @ w_in, SPLITS_EVEN, axis=-1)
    q = rms_norm(q.reshape(bsz, L, NA_HEADS, HEAD_DIM), q_norm)
    k = rms_norm(k.reshape(bsz, L, NA_HEADS, HEAD_DIM), k_norm)
    v = v.reshape(bsz, L, NA_HEADS, HEAD_DIM)
    return a, b, ga, q, k, v, gb


def even_output(y_conv, ga, y_att, gb, w_out):
    bsz, L = y_conv.shape[0], y_conv.shape[1]
    z = jnp.concatenate([y_conv * jax.nn.silu(ga),
                         y_att.reshape(bsz, L, D_ATT) * jax.nn.silu(gb)], axis=-1)
    return z @ w_out


def context_attention(q, k, v):
    bsz, L, H, Dh = q.shape
    qb = q.reshape(bsz, L // Q_BLOCK, Q_BLOCK, H, Dh).transpose(1, 0, 2, 3, 4)

    def block(qi):
        s = jnp.einsum('bqhd,bkhd->bhqk', qi, k).astype(jnp.float32) * ATT_SCALE
        p = jax.nn.softmax(s, axis=-1).astype(v.dtype)
        return jnp.einsum('bhqk,bkhd->bqhd', p, v)

    o = lax.map(block, qb)
    return o.transpose(1, 0, 2, 3, 4).reshape(bsz, L, H, Dh)


def neighbourhood_attention(q, k, v, k_ctx, v_ctx, rpb):
    bsz, L, H, Dh = q.shape
    rows = L // GRID_W
    kr = min(NA_ROWS, rows)
    ncb = GRID_W // NA_QCB
    r_all = jnp.arange(rows)
    row_start = jnp.clip(r_all - NA_ROWS // 2, 0, rows - kr)
    row_idx = row_start[:, None] + jnp.arange(kr)[None, :]
    cols = jnp.arange(GRID_W)
    q_col_start = jnp.clip(cols - NA_COLS // 2, 0, GRID_W - NA_COLS).reshape(ncb, NA_QCB)
    cb_start = jnp.clip(jnp.arange(ncb) * NA_QCB - NA_COLS // 2, 0, GRID_W - NA_KCB)
    col_idx = cb_start[:, None] + jnp.arange(NA_KCB)[None, :]
    key_col = col_idx[:, None, :]
    col_ok = (key_col >= q_col_start[:, :, None]) & (key_col < q_col_start[:, :, None] + NA_COLS)
    rel_c = key_col - cols.reshape(ncb, NA_QCB)[:, :, None]
    rel_c_idx = jnp.clip(rel_c + NA_COLS - 1, 0, 2 * NA_COLS - 2)
    rpb_c = rpb.astype(jnp.float32)[:, :, rel_c_idx]
    kg = k.reshape(bsz, rows, GRID_W, H, Dh)
    vg = v.reshape(bsz, rows, GRID_W, H, Dh)
    q_rows = q.reshape(bsz, rows, ncb, NA_QCB, H, Dh).transpose(1, 0, 2, 3, 4, 5)
    n_win = kr * NA_KCB

    def row_step(args):
        q_r, r_idx, r_id = args
        gidx = (r_idx[:, None, None], col_idx[None, :, :])
        kw = kg[:, gidx[0], gidx[1]]
        vw = vg[:, gidx[0], gidx[1]]
        s_win = jnp.einsum('bjqhd,bkjwhd->bhjqkw', q_r, kw).astype(jnp.float32) * ATT_SCALE
        bias = rpb_c[:, r_idx - r_id + NA_ROWS - 1].transpose(0, 2, 3, 1, 4)
        s_win = jnp.where(col_ok[None, :, :, None, :], s_win + bias, -jnp.inf)
        s_ctx = jnp.einsum('bjqhd,bmhd->bhjqm', q_r, k_ctx).astype(jnp.float32) * ATT_SCALE
        s = jnp.concatenate([s_win.reshape(bsz, H, ncb, NA_QCB, n_win), s_ctx], axis=-1)
        p = jax.nn.softmax(s, axis=-1).astype(v.dtype)
        p_win = p[..., :n_win].reshape(bsz, H, ncb, NA_QCB, kr, NA_KCB)
        p_ctx = p[..., n_win:]
        return (jnp.einsum('bhjqkw,bkjwhd->bjqhd', p_win, vw)
                + jnp.einsum('bhjqm,bmhd->bjqhd', p_ctx, v_ctx))

    o = lax.map(row_step, (q_rows, row_idx, r_all))
    return o.transpose(1, 0, 2, 3, 4, 5).reshape(bsz, L, H, Dh)


def multiscale_pool(u, pool_w, pool_scale):
    bsz, L, C = u.shape
    uf = u.astype(jnp.float32)
    csum = jnp.concatenate([jnp.zeros((bsz, 1, C), jnp.float32), jnp.cumsum(uf, axis=1)], axis=1)
    t = jnp.arange(L)
    outs = []
    for gi, w in enumerate(POOL_WINDOWS):
        lo = jnp.clip(t - w // 2, 0, L)
        hi = jnp.clip(t + w - w // 2, 0, L)
        sl = slice(gi * POOL_GD, (gi + 1) * POOL_GD)
        cg = csum[:, :, sl]
        mean = (cg[:, hi] - cg[:, lo]) / (hi - lo).astype(jnp.float32)[:, None]
        outs.append(mean - uf[:, :, sl])
    d = jnp.stack(outs, axis=2).astype(u.dtype)
    y = jnp.einsum('blgc,gce->blge', d, pool_w).reshape(bsz, L, C)
    return y * pool_scale


def odd_mixer(h, w_in, pool_w, pool_scale, w_out):
    u, g = jnp.split(h @ w_in, 2, axis=-1)
    return (multiscale_pool(u, pool_w, pool_scale) * jax.nn.silu(g)) @ w_out


def setup_inputs(seed: int = 0) -> dict:
    key = jax.random.key(seed)
    ks = jax.random.split(key, 32)

    def nrm(k, shape, s):
        return jax.random.normal(k, shape, jnp.float32) * s

    d = D_MODEL
    return {
        'x_prompt': nrm(ks[0], (BATCH, SEQ, d), 1.0),
        'x_sample': nrm(ks[1], (DEC_BATCH, DEC_SEQ, d), 1.0),
        'cache_k_0': nrm(ks[2], (DEC_BATCH, PAST_LEN, NA_HEADS, HEAD_DIM), 1.0),
        'cache_v_0': nrm(ks[3], (DEC_BATCH, PAST_LEN, NA_HEADS, HEAD_DIM), 1.0),
        'c': nrm(ks[4], (DEC_BATCH, d), 1.0),
        'c_ctx': nrm(ks[5], (d,), 1.0),
        'norm_g_0': 1.0 + nrm(ks[6], (d,), 0.02),
        'w_ada_0': nrm(ks[7], (d, 3 * d), 0.5 * d ** -0.5),
        'b_ada_0': nrm(ks[8], (3 * d,), 0.02),
        'w_in_0': nrm(ks[9], (d, D_IN_EVEN), d ** -0.5),
        'conv_w_0': nrm(ks[10], (CONV_K, D_CONV), CONV_K ** -0.5),
        'conv_b_0': nrm(ks[11], (D_CONV,), 0.01),
        'conv_ln_g_0': 1.0 + nrm(ks[12], (D_CONV,), 0.02),
        'conv_ln_b_0': nrm(ks[13], (D_CONV,), 0.01),
        'q_norm_0': 1.0 + nrm(ks[14], (HEAD_DIM,), 0.02),
        'k_norm_0': 1.0 + nrm(ks[15], (HEAD_DIM,), 0.02),
        'rpb_0': nrm(ks[16], (NA_HEADS, 2 * NA_ROWS - 1, 2 * NA_COLS - 1), 0.1),
        'w_out_0': nrm(ks[17], (D_CONV + D_ATT, d), (D_CONV + D_ATT) ** -0.5),
        'norm_g_1': 1.0 + nrm(ks[18], (d,), 0.02),
        'w_ada_1': nrm(ks[19], (d, 3 * d), 0.5 * d ** -0.5),
        'b_ada_1': nrm(ks[20], (3 * d,), 0.02),
        'w_in_1': nrm(ks[21], (d, 2 * D_POOL), d ** -0.5),
        'pool_w_1': nrm(ks[22], (POOL_GROUPS, POOL_GD, POOL_GD), POOL_GD ** -0.5),
        'pool_scale_1': 1.0 + nrm(ks[23], (D_POOL,), 0.02),
        'w_out_1': nrm(ks[24], (D_POOL, d), D_POOL ** -0.5),
    }


def reference(x_prompt, x_sample, cache_k_0, cache_v_0, c, c_ctx,
              norm_g_0, w_ada_0, b_ada_0, w_in_0, conv_w_0, conv_b_0, conv_ln_g_0, conv_ln_b_0,
              q_norm_0, k_norm_0, rpb_0, w_out_0,
              norm_g_1, w_ada_1, b_ada_1, w_in_1, pool_w_1, pool_scale_1, w_out_1):
    even_params = {0: (norm_g_0, w_ada_0, b_ada_0, w_in_0, conv_w_0, conv_b_0, conv_ln_g_0, conv_ln_b_0,
                       q_norm_0, k_norm_0, rpb_0, w_out_0)}
    odd_params = {1: (norm_g_1, w_ada_1, b_ada_1, w_in_1, pool_w_1, pool_scale_1, w_out_1)}
    caches = {0: (cache_k_0, cache_v_0)}
    cond_ctx = jnp.broadcast_to(c_ctx[None, :], (x_prompt.shape[0], D_MODEL))
    y_prompt, y_sample = x_prompt, x_sample
    new_state = {}
    for i in range(DEPTH):
        if i % 2 == 0:
            (norm_g, w_ada, b_ada, w_in, conv_w, conv_b, ln_g, ln_b, qn, kn, rpb, w_out) = even_params[i]
            h, gate = ada_modulate(y_prompt, cond_ctx, norm_g, w_ada, b_ada)
            a, b, ga, q, k, v, gb = even_projections(h, w_in, qn, kn)
            out = even_output(conformer_conv(a, b, conv_w, conv_b, ln_g, ln_b), ga,
                              context_attention(q, k, v), gb, w_out)
            y_prompt = y_prompt + gate * out
            new_state[i] = (k, v)
            h, gate = ada_modulate(y_sample, c, norm_g, w_ada, b_ada)
            a, b, ga, q, k, v, gb = even_projections(h, w_in, qn, kn)
            k_c, v_c = caches[i]
            out = even_output(conformer_conv(a, b, conv_w, conv_b, ln_g, ln_b), ga,
                              neighbourhood_attention(q, k, v, k_c, v_c, rpb), gb, w_out)
            y_sample = y_sample + gate * out
        else:
            (norm_g, w_ada, b_ada, w_in, pool_w, pool_scale, w_out) = odd_params[i]
            h, gate = ada_modulate(y_prompt, cond_ctx, norm_g, w_ada, b_ada)
            y_prompt = y_prompt + gate * odd_mixer(h, w_in, pool_w, pool_scale, w_out)
            h, gate = ada_modulate(y_sample, c, norm_g, w_ada, b_ada)
            y_sample = y_sample + gate * odd_mixer(h, w_in, pool_w, pool_scale, w_out)
    k_ctx_0, v_ctx_0 = new_state[0]
    return (y_prompt, y_sample, k_ctx_0, v_ctx_0)
```

```python
import functools

import jax
import jax.numpy as jnp
from jax import lax
from jax.experimental import pallas as pl
from jax.experimental.pallas import tpu as pltpu

F32 = jnp.float32
BF16 = jnp.bfloat16

D_MODEL = 1024
EPS = 1e-6
D_CONV = 512
CONV_K = 31
CONV_HALO = 16
NA_HEADS = 8
HEAD_DIM = 64
D_ATT = NA_HEADS * HEAD_DIM
HEADS_PER_VREG = 2
N_HEAD_PAIRS = NA_HEADS // HEADS_PER_VREG
PAIR_W = HEADS_PER_VREG * HEAD_DIM
GRID_W = 64
NA_ROWS = 8
NA_COLS = 16
ATT_SCALE = HEAD_DIM ** -0.5
D_IN_EVEN = 3 * D_CONV + 4 * D_ATT
POOL_WINDOWS = (2, 4, 8, 16)
POOL_GD = D_MODEL // len(POOL_WINDOWS)
POOL_HALO = 8
MASK_BIAS = -1e30
ADA_ROWS = 8
VMEM_LIMIT = 48 * 1024 * 1024


def _sigmoid(x):
    return 1.0 / (1.0 + jnp.exp(-x))


def _silu(x):
    return x * _sigmoid(x)


def _cparams(n_axes):
    return pltpu.CompilerParams(dimension_semantics=("parallel",) * n_axes,
                                vmem_limit_bytes=VMEM_LIMIT)


def _ada_kernel(cond_ref, w0_ref, b0_ref, w1_ref, b1_ref, m0_ref, m1_ref):
    s = _silu(cond_ref[...])
    m0_ref[...] = jnp.dot(s, w0_ref[...], preferred_element_type=F32,
                          precision=lax.Precision.HIGHEST) + b0_ref[...]
    m1_ref[...] = jnp.dot(s, w1_ref[...], preferred_element_type=F32,
                          precision=lax.Precision.HIGHEST) + b1_ref[...]


def _ada_modulation(cond, w_ada_0, b_ada_0, w_ada_1, b_ada_1):
    tn = 512
    n3 = 3 * D_MODEL
    wspec = pl.BlockSpec((D_MODEL, tn), lambda j: (0, j))
    bspec = pl.BlockSpec((1, tn), lambda j: (0, j))
    ospec = pl.BlockSpec((ADA_ROWS, tn), lambda j: (0, j))
    return pl.pallas_call(
        _ada_kernel,
        out_shape=(jax.ShapeDtypeStruct((ADA_ROWS, n3), F32),) * 2,
        grid=(n3 // tn,),
        in_specs=[pl.BlockSpec((ADA_ROWS, D_MODEL), lambda j: (0, 0)), wspec, bspec, wspec, bspec],
        out_specs=(ospec, ospec),
        compiler_params=_cparams(1),
        name="ada_modulation",
    )(cond, w_ada_0, b_ada_0.reshape(1, n3), w_ada_1, b_ada_1.reshape(1, n3))


def _mod_spec(mod, tiles_per_seq):
    n_mod = mod.shape[0]
    return pl.BlockSpec((1, 3, D_MODEL),
                        lambda i: (jnp.minimum(i // tiles_per_seq, n_mod - 1), 0, 0))


def _modulated_norm(x, g_row, mod_ref):
    ms = jnp.mean(x * x, axis=-1, keepdims=True)
    y = (x * lax.rsqrt(ms + EPS)) * g_row
    return y * (1.0 + mod_ref[0, 1:2, :]) + mod_ref[0, 0:1, :]


def _in0_kernel(x_ref, mod_ref, g_ref, w_ref, qn_ref, kn_ref, bd_ref,
                u_ref, sga_ref, q_ref, k_ref, v_ref, sgb_ref):
    hb = _modulated_norm(x_ref[...], g_ref[...], mod_ref).astype(BF16)

    def proj(c):
        return jnp.dot(hb, w_ref[:, c * D_CONV:(c + 1) * D_CONV], preferred_element_type=F32)

    def head_rms(t, g_row):
        ms = jnp.dot((t * t).astype(BF16), bd_ref[...], preferred_element_type=F32) * (1.0 / HEAD_DIM)
        return (t * lax.rsqrt(ms + EPS)) * g_row

    a = proj(0)
    u_ref[...] = a * _sigmoid(proj(1))
    sga_ref[...] = _silu(proj(2))
    q_ref[...] = (head_rms(proj(3), qn_ref[...]) * ATT_SCALE).astype(q_ref.dtype)
    k_ref[...] = head_rms(proj(4), kn_ref[...]).astype(k_ref.dtype)
    v_ref[...] = proj(5).astype(v_ref.dtype)
    sgb_ref[...] = _silu(proj(6))


def _layer0_input(x2, mod, L, tm, g0, w_in, qn, kn, bd, kv_dtype):
    n = x2.shape[0]
    tiles_per_seq = L // tm
    row = lambda i: (i, 0)
    const = lambda i: (0, 0)
    half = pl.BlockSpec((tm, D_CONV), row)
    return pl.pallas_call(
        _in0_kernel,
        out_shape=(jax.ShapeDtypeStruct((n, D_CONV), F32),
                   jax.ShapeDtypeStruct((n, D_CONV), F32),
                   jax.ShapeDtypeStruct((n, D_ATT), BF16),
                   jax.ShapeDtypeStruct((n, D_ATT), kv_dtype),
                   jax.ShapeDtypeStruct((n, D_ATT), kv_dtype),
                   jax.ShapeDtypeStruct((n, D_ATT), F32)),
        grid=(n // tm,),
        in_specs=[pl.BlockSpec((tm, D_MODEL), row),
                  _mod_spec(mod, tiles_per_seq),
                  pl.BlockSpec((1, D_MODEL), const),
                  pl.BlockSpec((D_MODEL, D_IN_EVEN), const),
                  pl.BlockSpec((1, D_ATT), const),
                  pl.BlockSpec((1, D_ATT), const),
                  pl.BlockSpec((D_ATT, D_ATT), const)],
        out_specs=(half,) * 6,
        compiler_params=_cparams(1),
        name="layer0_input",
    )(x2, mod, g0, w_in, qn, kn, bd)


def _conv_kernel(up_ref, u_ref, un_ref, sga_ref, w_ref, cb_ref, lg_ref, lb_ref, z_ref, buf_ref,
                 *, tile, n_tiles, chunk):
    i = pl.program_id(1)
    h = CONV_HALO
    buf_ref[0:h, :] = jnp.where(i > 0, up_ref[0], 0.0)
    buf_ref[h:h + tile, :] = u_ref[0]
    buf_ref[h + tile:2 * h + tile, :] = jnp.where(i < n_tiles - 1, un_ref[0], 0.0)
    off = h - CONV_K // 2

    for c in range(tile // chunk):
        base = c * chunk
        acc = jnp.zeros((chunk, D_CONV), F32)
        for k in range(CONV_K):
            acc = acc + buf_ref[base + off + k:base + off + k + chunk, :] * w_ref[k:k + 1, :]
        y = acc + cb_ref[...]
        mu = jnp.mean(y, axis=-1, keepdims=True)
        yc = y - mu
        var = jnp.mean(yc * yc, axis=-1, keepdims=True)
        yn = (yc * lax.rsqrt(var + EPS)) * lg_ref[...] + lb_ref[...]
        z = _silu(yn) * sga_ref[0, base:base + chunk, :]
        z_ref[0, base:base + chunk, :] = z.astype(z_ref.dtype)


def _halo_specs(tile, halo, seq, width):
    r = tile // halo
    last = seq // halo - 1
    prev = pl.BlockSpec((1, halo, width), lambda b, i: (b, jnp.maximum(i * r - 1, 0), 0))
    nxt = pl.BlockSpec((1, halo, width), lambda b, i: (b, jnp.minimum((i + 1) * r, last), 0))
    return prev, nxt


def _conv_module(u3, sga3, tile, conv_w, conv_b, ln_g, ln_b):
    bsz, L, _ = u3.shape
    n_tiles = L // tile
    main = pl.BlockSpec((1, tile, D_CONV), lambda b, i: (b, i, 0))
    prev, nxt = _halo_specs(tile, CONV_HALO, L, D_CONV)
    vec = pl.BlockSpec((1, D_CONV), lambda b, i: (0, 0))
    return pl.pallas_call(
        functools.partial(_conv_kernel, tile=tile, n_tiles=n_tiles, chunk=32),
        out_shape=jax.ShapeDtypeStruct((bsz, L, D_CONV), BF16),
        grid=(bsz, n_tiles),
        in_specs=[prev, main, nxt, main,
                  pl.BlockSpec((CONV_K, D_CONV), lambda b, i: (0, 0)), vec, vec, vec],
        out_specs=main,
        scratch_shapes=[pltpu.VMEM((tile + 2 * CONV_HALO, D_CONV), F32)],
        compiler_params=_cparams(2),
        name="conv_module",
    )(u3, u3, u3, sga3, conv_w, conv_b.reshape(1, D_CONV), ln_g.reshape(1, D_CONV),
      ln_b.reshape(1, D_CONV))


def _dot_nt(a, b):
    return lax.dot_general(a, b, (((1,), (1,)), ((), ())), preferred_element_type=F32)


def _head_masks():
    lane = lax.broadcasted_iota(jnp.int32, (1, PAIR_W), 1)
    first = lane < HEAD_DIM
    return (first, jnp.logical_not(first))


def _keep(mask, t):
    return jnp.where(mask, t, jnp.zeros_like(t))


def _ctx_attn_kernel(q_ref, k_ref, v_ref, sgb_ref, z_ref):
    q = q_ref[0]
    kb = k_ref[0].astype(BF16)
    vb = v_ref[0].astype(BF16)
    o = jnp.zeros(q.shape, F32)
    for hm in _head_masks():
        s = _dot_nt(_keep(hm, q), kb)
        p = jnp.exp(s - jnp.max(s, axis=-1, keepdims=True))
        l = jnp.sum(p, axis=-1, keepdims=True)
        o = o + jnp.dot(p.astype(BF16), _keep(hm, vb), preferred_element_type=F32) / l
    z_ref[0] = (o * sgb_ref[0]).astype(z_ref.dtype)


def _context_attention(q3, k3, v3, sgb3):
    bsz, L, _ = q3.shape
    blk = pl.BlockSpec((1, L, PAIR_W), lambda b, p: (b, 0, p))
    return pl.pallas_call(
        _ctx_attn_kernel,
        out_shape=jax.ShapeDtypeStruct((bsz, L, D_ATT), BF16),
        grid=(bsz, N_HEAD_PAIRS),
        in_specs=[blk, blk, blk, blk],
        out_specs=blk,
        compiler_params=_cparams(2),
        name="context_attention",
    )(q3, k3, v3, sgb3)


def _na_kernel(q_ref, k_ref, v_ref, kc_ref, vc_ref, bias_ref, sgb_ref, z_ref, *, rows_per_step, n_rows):
    rt = pl.program_id(2)
    masks = _head_masks()
    kc = kc_ref[0].astype(BF16)
    vc = vc_ref[0].astype(BF16)
    vc_heads = [_keep(hm, vc) for hm in masks]
    win = NA_ROWS * GRID_W

    def body(j, carry):
        r = rt * rows_per_step + j
        r_start = jnp.clip(r - NA_ROWS // 2, 0, n_rows - NA_ROWS)
        r_off = r - r_start
        qs = pl.ds(pl.multiple_of(j * GRID_W, GRID_W), GRID_W)
        ws = pl.ds(pl.multiple_of(r_start * GRID_W, GRID_W), win)
        q = q_ref[0, qs, :]
        kw = k_ref[0, ws, :]
        vw = v_ref[0, ws, :]
        o = jnp.zeros((GRID_W, PAIR_W), F32)
        for e, hm in enumerate(masks):
            qe = _keep(hm, q)
            s_win = _dot_nt(qe, kw) + bias_ref[r_off, e]
            s_ctx = _dot_nt(qe, kc)
            m = jnp.maximum(jnp.max(s_win, axis=-1, keepdims=True),
                            jnp.max(s_ctx, axis=-1, keepdims=True))
            p_win = jnp.exp(s_win - m)
            p_ctx = jnp.exp(s_ctx - m)
            l = jnp.sum(p_win, axis=-1, keepdims=True) + jnp.sum(p_ctx, axis=-1, keepdims=True)
            oe = (jnp.dot(p_win.astype(BF16), _keep(hm, vw), preferred_element_type=F32)
                  + jnp.dot(p_ctx.astype(BF16), vc_heads[e], preferred_element_type=F32))
            o = o + oe / l
        z_ref[0, qs, :] = (o * sgb_ref[0, qs, :]).astype(z_ref.dtype)
        return carry

    lax.fori_loop(0, rows_per_step, body, 0)


def _window_bias(rpb):
    qc = jnp.arange(GRID_W)
    kcol = jnp.arange(GRID_W)
    start = jnp.clip(qc - NA_COLS // 2, 0, GRID_W - NA_COLS)
    ok = (kcol[None, :] >= start[:, None]) & (kcol[None, :] < start[:, None] + NA_COLS)
    rel_c = jnp.clip(kcol[None, :] - qc[:, None] + NA_COLS - 1, 0, 2 * NA_COLS - 2)
    r_off = jnp.arange(NA_ROWS)
    rel_r = jnp.arange(NA_ROWS)[None, :] - r_off[:, None] + NA_ROWS - 1
    b = rpb.astype(F32)[:, rel_r][:, :, :, rel_c]
    b = jnp.where(ok[None, None, None], b, MASK_BIAS)
    b = b.transpose(1, 0, 3, 2, 4)
    return b.reshape(NA_ROWS, NA_HEADS, GRID_W, NA_ROWS * GRID_W)


def _neighbourhood_attention(q3, k3, v3, kc3, vc3, bias, sgb3, rows_per_step):
    bsz, L, _ = q3.shape
    n_rows = L // GRID_W
    past = kc3.shape[1]
    tq = rows_per_step * GRID_W
    qblk = pl.BlockSpec((1, tq, PAIR_W), lambda b, p, r: (b, r, p))
    full = pl.BlockSpec((1, L, PAIR_W), lambda b, p, r: (b, 0, p))
    ctx = pl.BlockSpec((1, past, PAIR_W), lambda b, p, r: (b, 0, p))
    bias_spec = pl.BlockSpec((NA_ROWS, HEADS_PER_VREG, GRID_W, NA_ROWS * GRID_W),
                             lambda b, p, r: (0, p, 0, 0))
    return pl.pallas_call(
        functools.partial(_na_kernel, rows_per_step=rows_per_step, n_rows=n_rows),
        out_shape=jax.ShapeDtypeStruct((bsz, L, D_ATT), BF16),
        grid=(bsz, N_HEAD_PAIRS, n_rows // rows_per_step),
        in_specs=[qblk, full, full, ctx, ctx, bias_spec, qblk],
        out_specs=qblk,
        compiler_params=_cparams(3),
        name="neighbourhood_attention",
    )(q3, k3, v3, kc3, vc3, bias, sgb3)


def _mid_kernel(x_ref, zc_ref, za_ref, mod0_ref, mod1_ref, g1_ref, wo_ref, w1_ref,
                y_ref, u1_ref, sg1_ref):
    out = (jnp.dot(zc_ref[...], wo_ref[0:D_CONV, :], preferred_element_type=F32)
           + jnp.dot(za_ref[...], wo_ref[D_CONV:D_CONV + D_ATT, :], preferred_element_type=F32))
    y = x_ref[...] + mod0_ref[0, 2:3, :] * out
    y_ref[...] = y
    hb = _modulated_norm(y, g1_ref[...], mod1_ref).astype(BF16)
    u1_ref[...] = jnp.dot(hb, w1_ref[:, 0:D_MODEL], preferred_element_type=F32)
    sg1_ref[...] = _silu(jnp.dot(hb, w1_ref[:, D_MODEL:2 * D_MODEL], preferred_element_type=F32))


def _layer0_output_layer1_input(x2, zc2, za2, mod0, mod1, L, tm, g1, w_out0, w_in1):
    n = x2.shape[0]
    tiles_per_seq = L // tm
    row = lambda i: (i, 0)
    const = lambda i: (0, 0)
    modspec = _mod_spec(mod0, tiles_per_seq)
    full = pl.BlockSpec((tm, D_MODEL), row)
    half = pl.BlockSpec((tm, D_CONV), row)
    return pl.pallas_call(
        _mid_kernel,
        out_shape=(jax.ShapeDtypeStruct((n, D_MODEL), F32),) * 3,
        grid=(n // tm,),
        in_specs=[full, half, half, modspec, modspec,
                  pl.BlockSpec((1, D_MODEL), const),
                  pl.BlockSpec((D_CONV + D_ATT, D_MODEL), const),
                  pl.BlockSpec((D_MODEL, 2 * D_MODEL), const)],
        out_specs=(full, full, full),
        compiler_params=_cparams(1),
        name="layer0_output_layer1_input",
    )(x2, zc2, za2, mod0, mod1, g1, w_out0, w_in1)


def _out1_kernel(up_ref, u_ref, un_ref, sg_ref, y_ref, mod_ref, pw_ref, ps_ref, wo_ref, o_ref, buf_ref,
                 *, tile, n_tiles, seq):
    i = pl.program_id(1)
    h = POOL_HALO
    buf_ref[0:h, :] = jnp.where(i > 0, up_ref[0], 0.0)
    buf_ref[h:h + tile, :] = u_ref[0]
    buf_ref[h + tile:2 * h + tile, :] = jnp.where(i < n_tiles - 1, un_ref[0], 0.0)
    t = i * tile + lax.broadcasted_iota(jnp.int32, (tile, 1), 0)
    out = jnp.zeros((tile, D_MODEL), F32)
    for g, w in enumerate(POOL_WINDOWS):
        cols = slice(g * POOL_GD, (g + 1) * POOL_GD)
        s = buf_ref[h - w // 2:h - w // 2 + tile, cols]
        for j in range(1 - w // 2, w - w // 2):
            s = s + buf_ref[h + j:h + j + tile, cols]
        lo = jnp.clip(t - w // 2, 0, seq)
        hi = jnp.clip(t + w - w // 2, 0, seq)
        d = s / (hi - lo).astype(F32) - buf_ref[h:h + tile, cols]
        yp = jnp.dot(d.astype(BF16), pw_ref[g], preferred_element_type=F32) * ps_ref[:, cols]
        z = (yp * sg_ref[0, :, cols]).astype(BF16)
        out = out + jnp.dot(z, wo_ref[cols, :], preferred_element_type=F32)
    o_ref[0] = y_ref[0] + mod_ref[0, 2:3, :] * out


def _layer1_output(u3, sg3, y3, mod1, tile, pool_w, pool_scale, w_out1):
    bsz, L, _ = u3.shape
    n_tiles = L // tile
    main = pl.BlockSpec((1, tile, D_MODEL), lambda b, i: (b, i, 0))
    prev, nxt = _halo_specs(tile, POOL_HALO, L, D_MODEL)
    n_mod = mod1.shape[0]
    modspec = pl.BlockSpec((1, 3, D_MODEL), lambda b, i: (jnp.minimum(b, n_mod - 1), 0, 0))
    return pl.pallas_call(
        functools.partial(_out1_kernel, tile=tile, n_tiles=n_tiles, seq=L),
        out_shape=jax.ShapeDtypeStruct((bsz, L, D_MODEL), F32),
        grid=(bsz, n_tiles),
        in_specs=[prev, main, nxt, main, main, modspec,
                  pl.BlockSpec((len(POOL_WINDOWS), POOL_GD, POOL_GD), lambda b, i: (0, 0, 0)),
                  pl.BlockSpec((1, D_MODEL), lambda b, i: (0, 0)),
                  pl.BlockSpec((D_MODEL, D_MODEL), lambda b, i: (0, 0))],
        out_specs=main,
        scratch_shapes=[pltpu.VMEM((tile + 2 * POOL_HALO, D_MODEL), F32)],
        compiler_params=_cparams(2),
        name="layer1_output",
    )(u3, u3, u3, sg3, y3, mod1, pool_w, pool_scale.reshape(1, D_MODEL), w_out1)


def _path(x, mod0, mod1, tm, conv_tile, pool_tile, params, attention):
    (g0, w_in0, conv_w, conv_b, ln_g, ln_b, qn, kn, bd, w_out0, g1, w_in1, pool_w, pool_scale,
     w_out1, kv_dtype) = params
    bsz, L, _ = x.shape
    x2 = x.reshape(bsz * L, D_MODEL)
    u, sga, q, k, v, sgb = _layer0_input(x2, mod0, L, tm, g0, w_in0, qn, kn, bd, kv_dtype)
    r3 = lambda t: t.reshape(bsz, L, t.shape[-1])
    zc = _conv_module(r3(u), r3(sga), conv_tile, conv_w, conv_b, ln_g, ln_b)
    za = attention(r3(q), r3(k), r3(v), r3(sgb))
    y, u1, sg1 = _layer0_output_layer1_input(x2, zc.reshape(bsz * L, D_CONV), za.reshape(bsz * L, D_ATT),
                                             mod0, mod1, L, tm, g1, w_out0, w_in1)
    out = _layer1_output(r3(u1), r3(sg1), r3(y), mod1, pool_tile, pool_w, pool_scale, w_out1)
    return out, k, v


def kernel(x_prompt, x_sample, cache_k_0, cache_v_0, c, c_ctx, norm_g_0, w_ada_0, b_ada_0, w_in_0, conv_w_0, conv_b_0, conv_ln_g_0, conv_ln_b_0, q_norm_0, k_norm_0, rpb_0, w_out_0, norm_g_1, w_ada_1, b_ada_1, w_in_1, pool_w_1, pool_scale_1, w_out_1):
    bp, lp, _ = x_prompt.shape
    bs, ls, _ = x_sample.shape
    past = cache_k_0.shape[1]

    cond = jnp.concatenate([c_ctx[None, :], c, jnp.zeros((ADA_ROWS - 1 - bs, D_MODEL), F32)], axis=0)
    m0, m1 = _ada_modulation(cond, w_ada_0, b_ada_0, w_ada_1, b_ada_1)
    split = lambda m, lo, hi: m[lo:hi].reshape(hi - lo, 3, D_MODEL)
    mod0_p, mod1_p = split(m0, 0, 1), split(m1, 0, 1)
    mod0_s, mod1_s = split(m0, 1, 1 + bs), split(m1, 1, 1 + bs)

    head_id = jnp.arange(D_ATT) // HEAD_DIM
    bd = (head_id[:, None] == head_id[None, :]).astype(BF16)
    common = (norm_g_0.reshape(1, D_MODEL), w_in_0.astype(BF16), conv_w_0, conv_b_0, conv_ln_g_0,
              conv_ln_b_0, jnp.tile(q_norm_0, NA_HEADS).reshape(1, D_ATT),
              jnp.tile(k_norm_0, NA_HEADS).reshape(1, D_ATT), bd, w_out_0.astype(BF16),
              norm_g_1.reshape(1, D_MODEL), w_in_1.astype(BF16), pool_w_1.astype(BF16), pool_scale_1,
              w_out_1.astype(BF16))

    y_prompt, k_ctx, v_ctx = _path(x_prompt, mod0_p, mod1_p, 256, lp, lp, common + (F32,),
                                   _context_attention)

    bias = _window_bias(rpb_0)
    kc3 = cache_k_0.reshape(bs, past, D_ATT)
    vc3 = cache_v_0.reshape(bs, past, D_ATT)
    na = lambda q3, k3, v3, sgb3: _neighbourhood_attention(q3, k3, v3, kc3, vc3, bias, sgb3, 8)
    y_sample, _, _ = _path(x_sample, mod0_s, mod1_s, 256, 512, 512, common + (BF16,), na)

    return (y_prompt, y_sample,
            k_ctx.reshape(bp, lp, NA_HEADS, HEAD_DIM), v_ctx.reshape(bp, lp, NA_HEADS, HEAD_DIM))
```

```python
import functools

import jax
import jax.numpy as jnp
from jax import lax
from jax.experimental import pallas as pl
from jax.experimental.pallas import tpu as pltpu

F32 = jnp.float32
BF16 = jnp.bfloat16

D_MODEL = 1024
EPS = 1e-6
D_CONV = 512
CONV_K = 31
CONV_HALO = 16
NA_HEADS = 8
HEAD_DIM = 64
D_ATT = NA_HEADS * HEAD_DIM
HEADS_PER_VREG = 2
N_HEAD_PAIRS = NA_HEADS // HEADS_PER_VREG
PAIR_W = HEADS_PER_VREG * HEAD_DIM
GRID_W = 64
NA_ROWS = 8
NA_COLS = 16
ATT_SCALE = HEAD_DIM ** -0.5
D_IN_EVEN = 3 * D_CONV + 4 * D_ATT
POOL_WINDOWS = (2, 4, 8, 16)
POOL_GD = D_MODEL // len(POOL_WINDOWS)
POOL_HALO = 8
MASK_BIAS = -1e30
SUBLANES = 8
ADA_ROWS = SUBLANES
VMEM_LIMIT = 48 * 1024 * 1024


def _sigmoid(x):
    return 1.0 / (1.0 + jnp.exp(-x))


def _silu(x):
    return x * _sigmoid(x)


def _cparams(n_axes):
    return pltpu.CompilerParams(dimension_semantics=("parallel",) * n_axes,
                                vmem_limit_bytes=VMEM_LIMIT)


def _ada_kernel(cond_ref, w0_ref, b0_ref, w1_ref, b1_ref, m0_ref, m1_ref):
    s = _silu(cond_ref[...])
    m0_ref[...] = jnp.dot(s, w0_ref[...], preferred_element_type=F32,
                          precision=lax.Precision.HIGHEST) + b0_ref[...]
    m1_ref[...] = jnp.dot(s, w1_ref[...], preferred_element_type=F32,
                          precision=lax.Precision.HIGHEST) + b1_ref[...]


def _ada_modulation(cond, w_ada_0, b_ada_0, w_ada_1, b_ada_1):
    tn = 512
    n3 = 3 * D_MODEL
    wspec = pl.BlockSpec((D_MODEL, tn), lambda j: (0, j))
    bspec = pl.BlockSpec((1, tn), lambda j: (0, j))
    ospec = pl.BlockSpec((ADA_ROWS, tn), lambda j: (0, j))
    return pl.pallas_call(
        _ada_kernel,
        out_shape=(jax.ShapeDtypeStruct((ADA_ROWS, n3), F32),) * 2,
        grid=(n3 // tn,),
        in_specs=[pl.BlockSpec((ADA_ROWS, D_MODEL), lambda j: (0, 0)), wspec, bspec, wspec, bspec],
        out_specs=(ospec, ospec),
        compiler_params=_cparams(1),
        name="ada_modulation",
    )(cond, w_ada_0, b_ada_0.reshape(1, n3), w_ada_1, b_ada_1.reshape(1, n3))


def _mod_spec(mod, tiles_per_seq):
    n_mod = mod.shape[0]
    return pl.BlockSpec((1, 3, D_MODEL),
                        lambda i: (jnp.minimum(i // tiles_per_seq, n_mod - 1), 0, 0))


def _modulated_norm(x, g_row, mod_ref):
    ms = jnp.mean(x * x, axis=-1, keepdims=True)
    y = (x * lax.rsqrt(ms + EPS)) * g_row
    return y * (1.0 + mod_ref[0, 1:2, :]) + mod_ref[0, 0:1, :]


def _in0_kernel(x_ref, mod_ref, g_ref, w_ref, qn_ref, kn_ref, bd_ref,
                u_ref, sga_ref, q_ref, k_ref, v_ref, sgb_ref):
    hb = _modulated_norm(x_ref[...], g_ref[...], mod_ref).astype(BF16)

    def proj(c):
        return jnp.dot(hb, w_ref[:, c * D_CONV:(c + 1) * D_CONV], preferred_element_type=F32)

    def head_rms(t, g_row):
        ms = jnp.dot((t * t).astype(BF16), bd_ref[...], preferred_element_type=F32) * (1.0 / HEAD_DIM)
        return (t * lax.rsqrt(ms + EPS)) * g_row

    a = proj(0)
    u_ref[...] = a * _sigmoid(proj(1))
    sga_ref[...] = _silu(proj(2))
    q_ref[...] = (head_rms(proj(3), qn_ref[...]) * ATT_SCALE).astype(q_ref.dtype)
    k_ref[...] = head_rms(proj(4), kn_ref[...]).astype(k_ref.dtype)
    v_ref[...] = proj(5).astype(v_ref.dtype)
    sgb_ref[...] = _silu(proj(6))


def _layer0_input(x2, mod, L, tm, g0, w_in, qn, kn, bd, kv_dtype):
    n = x2.shape[0]
    tiles_per_seq = L // tm
    row = lambda i: (i, 0)
    const = lambda i: (0, 0)
    half = pl.BlockSpec((tm, D_CONV), row)
    return pl.pallas_call(
        _in0_kernel,
        out_shape=(jax.ShapeDtypeStruct((n, D_CONV), F32),
                   jax.ShapeDtypeStruct((n, D_CONV), F32),
                   jax.ShapeDtypeStruct((n, D_ATT), BF16),
                   jax.ShapeDtypeStruct((n, D_ATT), kv_dtype),
                   jax.ShapeDtypeStruct((n, D_ATT), kv_dtype),
                   jax.ShapeDtypeStruct((n, D_ATT), F32)),
        grid=(n // tm,),
        in_specs=[pl.BlockSpec((tm, D_MODEL), row),
                  _mod_spec(mod, tiles_per_seq),
                  pl.BlockSpec((1, D_MODEL), const),
                  pl.BlockSpec((D_MODEL, D_IN_EVEN), const),
                  pl.BlockSpec((1, D_ATT), const),
                  pl.BlockSpec((1, D_ATT), const),
                  pl.BlockSpec((D_ATT, D_ATT), const)],
        out_specs=(half,) * 6,
        compiler_params=_cparams(1),
        name="layer0_input",
    )(x2, mod, g0, w_in, qn, kn, bd)


def _conv_kernel(up_ref, u_ref, un_ref, sga_ref, w_ref, cb_ref, lg_ref, lb_ref, z_ref, buf_ref, sh_ref,
                 *, tile, n_tiles, chunk):
    i = pl.program_id(1)
    h = CONV_HALO
    buf_ref[0:h, :] = jnp.where(i > 0, up_ref[0], 0.0)
    buf_ref[h:h + tile, :] = u_ref[0]
    buf_ref[h + tile:2 * h + tile, :] = jnp.where(i < n_tiles - 1, un_ref[0], 0.0)
    off = h - CONV_K // 2
    span = sh_ref.shape[1]
    for s in range(SUBLANES):
        sh_ref[s] = buf_ref[s:s + span, :]

    for c in range(tile // chunk):
        base = c * chunk
        acc = jnp.zeros((chunk, D_CONV), F32)
        for k in range(CONV_K):
            s, a = (off + k) % SUBLANES, (off + k) // SUBLANES * SUBLANES
            acc = acc + sh_ref[s, base + a:base + a + chunk, :] * w_ref[k:k + 1, :]
        y = acc + cb_ref[...]
        mu = jnp.mean(y, axis=-1, keepdims=True)
        yc = y - mu
        var = jnp.mean(yc * yc, axis=-1, keepdims=True)
        yn = (yc * lax.rsqrt(var + EPS)) * lg_ref[...] + lb_ref[...]
        z = _silu(yn) * sga_ref[0, base:base + chunk, :]
        z_ref[0, base:base + chunk, :] = z.astype(z_ref.dtype)


def _halo_specs(tile, halo, seq, width):
    r = tile // halo
    last = seq // halo - 1
    prev = pl.BlockSpec((1, halo, width), lambda b, i: (b, jnp.maximum(i * r - 1, 0), 0))
    nxt = pl.BlockSpec((1, halo, width), lambda b, i: (b, jnp.minimum((i + 1) * r, last), 0))
    return prev, nxt


def _conv_module(u3, sga3, tile, conv_w, conv_b, ln_g, ln_b):
    bsz, L, _ = u3.shape
    n_tiles = L // tile
    main = pl.BlockSpec((1, tile, D_CONV), lambda b, i: (b, i, 0))
    prev, nxt = _halo_specs(tile, CONV_HALO, L, D_CONV)
    vec = pl.BlockSpec((1, D_CONV), lambda b, i: (0, 0))
    return pl.pallas_call(
        functools.partial(_conv_kernel, tile=tile, n_tiles=n_tiles, chunk=32),
        out_shape=jax.ShapeDtypeStruct((bsz, L, D_CONV), BF16),
        grid=(bsz, n_tiles),
        in_specs=[prev, main, nxt, main,
                  pl.BlockSpec((CONV_K, D_CONV), lambda b, i: (0, 0)), vec, vec, vec],
        out_specs=main,
        scratch_shapes=[pltpu.VMEM((tile + 2 * CONV_HALO, D_CONV), F32),
                        pltpu.VMEM((SUBLANES, tile + 2 * CONV_HALO - SUBLANES, D_CONV), F32)],
        compiler_params=_cparams(2),
        name="conv_module",
    )(u3, u3, u3, sga3, conv_w, conv_b.reshape(1, D_CONV), ln_g.reshape(1, D_CONV),
      ln_b.reshape(1, D_CONV))


def _dot_nt(a, b):
    return lax.dot_general(a, b, (((1,), (1,)), ((), ())), preferred_element_type=F32)


def _head_masks():
    lane = lax.broadcasted_iota(jnp.int32, (1, PAIR_W), 1)
    first = lane < HEAD_DIM
    return (first, jnp.logical_not(first))


def _keep(mask, t):
    return jnp.where(mask, t, jnp.zeros_like(t))


def _ctx_attn_kernel(q_ref, k_ref, v_ref, sgb_ref, z_ref):
    masks = _head_masks()

    def cols(p):
        return slice(p * PAIR_W, (p + 1) * PAIR_W)

    def scores(p, e):
        return _dot_nt(_keep(masks[e], q_ref[0, :, cols(p)]), k_ref[0, :, cols(p)].astype(BF16))

    def finish(p, e, s):
        pr = jnp.exp(s - jnp.max(s, axis=-1, keepdims=True))
        l = jnp.sum(pr, axis=-1, keepdims=True)
        vb = _keep(masks[e], v_ref[0, :, cols(p)].astype(BF16))
        return jnp.dot(pr.astype(BF16), vb, preferred_element_type=F32) / l

    chains = [(p, e) for p in range(N_HEAD_PAIRS) for e in range(HEADS_PER_VREG)]
    done = {}
    s_next = scores(*chains[0])
    for c, (p, e) in enumerate(chains):
        s_cur = s_next
        if c + 1 < len(chains):
            s_next = scores(*chains[c + 1])
        done[(p, e)] = finish(p, e, s_cur)
    for p in range(N_HEAD_PAIRS):
        o = done[(p, 0)] + done[(p, 1)]
        z_ref[0, :, cols(p)] = (o * sgb_ref[0, :, cols(p)]).astype(z_ref.dtype)


def _context_attention(q3, k3, v3, sgb3):
    bsz, L, _ = q3.shape
    blk = pl.BlockSpec((1, L, D_ATT), lambda b: (b, 0, 0))
    return pl.pallas_call(
        _ctx_attn_kernel,
        out_shape=jax.ShapeDtypeStruct((bsz, L, D_ATT), BF16),
        grid=(bsz,),
        in_specs=[blk, blk, blk, blk],
        out_specs=blk,
        compiler_params=_cparams(1),
        name="context_attention",
    )(q3, k3, v3, sgb3)


def _na_kernel(q_ref, k_ref, v_ref, kc_ref, vc_ref, bias_ref, sgb_ref, z_ref, *, rows_per_step, n_rows):
    rt = pl.program_id(2)
    masks = _head_masks()
    kc = kc_ref[0].astype(BF16)
    vc = vc_ref[0].astype(BF16)
    q_all = q_ref[0]
    win = NA_ROWS * GRID_W
    q_heads = [_keep(hm, q_all) for hm in masks]
    s_ctx = [_dot_nt(qe, kc) for qe in q_heads]

    def window(j):
        r = rt * rows_per_step + j
        r_start = jnp.clip(r - NA_ROWS // 2, 0, n_rows - NA_ROWS)
        ws = pl.ds(pl.multiple_of(r_start * GRID_W, GRID_W), win)
        return r - r_start, ws

    def scores(j, e):
        r_off, ws = window(j)
        return _dot_nt(q_heads[e][j * GRID_W:(j + 1) * GRID_W], k_ref[0, ws, :]) + bias_ref[r_off, e]

    def finish(j, e, s_win):
        _, ws = window(j)
        sc = s_ctx[e][j * GRID_W:(j + 1) * GRID_W]
        m = jnp.maximum(jnp.max(s_win, axis=-1, keepdims=True), jnp.max(sc, axis=-1, keepdims=True))
        p_win = jnp.exp(s_win - m)
        p_ctx = jnp.exp(sc - m)
        l = jnp.sum(p_win, axis=-1, keepdims=True) + jnp.sum(p_ctx, axis=-1, keepdims=True)
        o_win = jnp.dot(p_win.astype(BF16), _keep(masks[e], v_ref[0, ws, :]), preferred_element_type=F32)
        return o_win, p_ctx.astype(BF16), l

    chains = [(j, e) for j in range(rows_per_step) for e in range(HEADS_PER_VREG)]
    done = {}
    s_next = scores(*chains[0])
    for c, (j, e) in enumerate(chains):
        s_cur = s_next
        if c + 1 < len(chains):
            s_next = scores(*chains[c + 1])
        done[(j, e)] = finish(j, e, s_cur)

    o = jnp.zeros(q_all.shape, F32)
    for e, hm in enumerate(masks):
        rows = [done[(j, e)] for j in range(rows_per_step)]
        o_win = jnp.concatenate([t[0] for t in rows], axis=0)
        p_ctx = jnp.concatenate([t[1] for t in rows], axis=0)
        l = jnp.concatenate([t[2] for t in rows], axis=0)
        o = o + (o_win + jnp.dot(p_ctx, _keep(hm, vc), preferred_element_type=F32)) / l
    z_ref[0] = (o * sgb_ref[0]).astype(z_ref.dtype)


def _window_bias(rpb):
    qc = jnp.arange(GRID_W)
    kcol = jnp.arange(GRID_W)
    start = jnp.clip(qc - NA_COLS // 2, 0, GRID_W - NA_COLS)
    ok = (kcol[None, :] >= start[:, None]) & (kcol[None, :] < start[:, None] + NA_COLS)
    rel_c = kcol[None, :] - qc[:, None] + NA_COLS - 1
    r_off = jnp.arange(NA_ROWS)
    rel_r = jnp.arange(NA_ROWS)[None, :] - r_off[:, None] + NA_ROWS - 1
    onehot = ((rel_c[None] == jnp.arange(2 * NA_COLS - 1)[:, None, None]) & ok[None]).astype(F32)
    rows = rpb.astype(F32)[:, rel_r]
    b = jnp.einsum('hrjd,dqk->rhqjk', rows, onehot, precision=lax.Precision.HIGHEST)
    b = jnp.where(ok[None, None, :, None, :], b, MASK_BIAS)
    return b.reshape(NA_ROWS, NA_HEADS, GRID_W, NA_ROWS * GRID_W)


def _neighbourhood_attention(q3, k3, v3, kc3, vc3, bias, sgb3, rows_per_step):
    bsz, L, _ = q3.shape
    n_rows = L // GRID_W
    past = kc3.shape[1]
    tq = rows_per_step * GRID_W
    qblk = pl.BlockSpec((1, tq, PAIR_W), lambda b, p, r: (b, r, p))
    full = pl.BlockSpec((1, L, PAIR_W), lambda b, p, r: (b, 0, p))
    ctx = pl.BlockSpec((1, past, PAIR_W), lambda b, p, r: (b, 0, p))
    bias_spec = pl.BlockSpec((NA_ROWS, HEADS_PER_VREG, GRID_W, NA_ROWS * GRID_W),
                             lambda b, p, r: (0, p, 0, 0))
    return pl.pallas_call(
        functools.partial(_na_kernel, rows_per_step=rows_per_step, n_rows=n_rows),
        out_shape=jax.ShapeDtypeStruct((bsz, L, D_ATT), BF16),
        grid=(bsz, N_HEAD_PAIRS, n_rows // rows_per_step),
        in_specs=[qblk, full, full, ctx, ctx, bias_spec, qblk],
        out_specs=qblk,
        compiler_params=_cparams(3),
        name="neighbourhood_attention",
    )(q3, k3, v3, kc3, vc3, bias, sgb3)


def _mid_kernel(x_ref, zc_ref, za_ref, mod0_ref, mod1_ref, g1_ref, wo_ref, w1_ref,
                y_ref, u1_ref, sg1_ref):
    out = (jnp.dot(zc_ref[...], wo_ref[0:D_CONV, :], preferred_element_type=F32)
           + jnp.dot(za_ref[...], wo_ref[D_CONV:D_CONV + D_ATT, :], preferred_element_type=F32))
    y = x_ref[...] + mod0_ref[0, 2:3, :] * out
    y_ref[...] = y
    hb = _modulated_norm(y, g1_ref[...], mod1_ref).astype(BF16)
    u1_ref[...] = jnp.dot(hb, w1_ref[:, 0:D_MODEL], preferred_element_type=F32)
    sg1_ref[...] = _silu(jnp.dot(hb, w1_ref[:, D_MODEL:2 * D_MODEL], preferred_element_type=F32))


def _layer0_output_layer1_input(x2, zc2, za2, mod0, mod1, L, tm, g1, w_out0, w_in1):
    n = x2.shape[0]
    tiles_per_seq = L // tm
    row = lambda i: (i, 0)
    const = lambda i: (0, 0)
    modspec = _mod_spec(mod0, tiles_per_seq)
    full = pl.BlockSpec((tm, D_MODEL), row)
    half = pl.BlockSpec((tm, D_CONV), row)
    return pl.pallas_call(
        _mid_kernel,
        out_shape=(jax.ShapeDtypeStruct((n, D_MODEL), F32),) * 3,
        grid=(n // tm,),
        in_specs=[full, half, half, modspec, modspec,
                  pl.BlockSpec((1, D_MODEL), const),
                  pl.BlockSpec((D_CONV + D_ATT, D_MODEL), const),
                  pl.BlockSpec((D_MODEL, 2 * D_MODEL), const)],
        out_specs=(full, full, full),
        compiler_params=_cparams(1),
        name="layer0_output_layer1_input",
    )(x2, zc2, za2, mod0, mod1, g1, w_out0, w_in1)


def _out1_kernel(up_ref, u_ref, un_ref, sg_ref, y_ref, mod_ref, pw_ref, ps_ref, wo_ref, o_ref, buf_ref,
                 *, tile, n_tiles, seq):
    i = pl.program_id(1)
    h = POOL_HALO
    buf_ref[0:h, :] = jnp.where(i > 0, up_ref[0], 0.0)
    buf_ref[h:h + tile, :] = u_ref[0]
    buf_ref[h + tile:2 * h + tile, :] = jnp.where(i < n_tiles - 1, un_ref[0], 0.0)
    t = i * tile + lax.broadcasted_iota(jnp.int32, (tile, 1), 0)
    out = jnp.zeros((tile, D_MODEL), F32)
    for g, w in enumerate(POOL_WINDOWS):
        cols = slice(g * POOL_GD, (g + 1) * POOL_GD)
        s = buf_ref[h - w // 2:h - w // 2 + tile, cols]
        for j in range(1 - w // 2, w - w // 2):
            s = s + buf_ref[h + j:h + j + tile, cols]
        lo = jnp.clip(t - w // 2, 0, seq)
        hi = jnp.clip(t + w - w // 2, 0, seq)
        d = s / (hi - lo).astype(F32) - buf_ref[h:h + tile, cols]
        yp = jnp.dot(d.astype(BF16), pw_ref[g], preferred_element_type=F32) * ps_ref[:, cols]
        z = (yp * sg_ref[0, :, cols]).astype(BF16)
        out = out + jnp.dot(z, wo_ref[cols, :], preferred_element_type=F32)
    o_ref[0] = y_ref[0] + mod_ref[0, 2:3, :] * out


def _layer1_output(u3, sg3, y3, mod1, tile, pool_w, pool_scale, w_out1):
    bsz, L, _ = u3.shape
    n_tiles = L // tile
    main = pl.BlockSpec((1, tile, D_MODEL), lambda b, i: (b, i, 0))
    prev, nxt = _halo_specs(tile, POOL_HALO, L, D_MODEL)
    n_mod = mod1.shape[0]
    modspec = pl.BlockSpec((1, 3, D_MODEL), lambda b, i: (jnp.minimum(b, n_mod - 1), 0, 0))
    return pl.pallas_call(
        functools.partial(_out1_kernel, tile=tile, n_tiles=n_tiles, seq=L),
        out_shape=jax.ShapeDtypeStruct((bsz, L, D_MODEL), F32),
        grid=(bsz, n_tiles),
        in_specs=[prev, main, nxt, main, main, modspec,
                  pl.BlockSpec((len(POOL_WINDOWS), POOL_GD, POOL_GD), lambda b, i: (0, 0, 0)),
                  pl.BlockSpec((1, D_MODEL), lambda b, i: (0, 0)),
                  pl.BlockSpec((D_MODEL, D_MODEL), lambda b, i: (0, 0))],
        out_specs=main,
        scratch_shapes=[pltpu.VMEM((tile + 2 * POOL_HALO, D_MODEL), F32)],
        compiler_params=_cparams(2),
        name="layer1_output",
    )(u3, u3, u3, sg3, y3, mod1, pool_w, pool_scale.reshape(1, D_MODEL), w_out1)


def _path(x, mod0, mod1, tm, conv_tile, pool_tile, params, attention):
    (g0, w_in0, conv_w, conv_b, ln_g, ln_b, qn, kn, bd, w_out0, g1, w_in1, pool_w, pool_scale,
     w_out1, kv_dtype) = params
    bsz, L, _ = x.shape
    x2 = x.reshape(bsz * L, D_MODEL)
    u, sga, q, k, v, sgb = _layer0_input(x2, mod0, L, tm, g0, w_in0, qn, kn, bd, kv_dtype)
    r3 = lambda t: t.reshape(bsz, L, t.shape[-1])
    zc = _conv_module(r3(u), r3(sga), conv_tile, conv_w, conv_b, ln_g, ln_b)
    za = attention(r3(q), r3(k), r3(v), r3(sgb))
    y, u1, sg1 = _layer0_output_layer1_input(x2, zc.reshape(bsz * L, D_CONV), za.reshape(bsz * L, D_ATT),
                                             mod0, mod1, L, tm, g1, w_out0, w_in1)
    out = _layer1_output(r3(u1), r3(sg1), r3(y), mod1, pool_tile, pool_w, pool_scale, w_out1)
    return out, k, v


def kernel(x_prompt, x_sample, cache_k_0, cache_v_0, c, c_ctx, norm_g_0, w_ada_0, b_ada_0, w_in_0, conv_w_0, conv_b_0, conv_ln_g_0, conv_ln_b_0, q_norm_0, k_norm_0, rpb_0, w_out_0, norm_g_1, w_ada_1, b_ada_1, w_in_1, pool_w_1, pool_scale_1, w_out_1):
    bp, lp, _ = x_prompt.shape
    bs, ls, _ = x_sample.shape
    past = cache_k_0.shape[1]

    cond = jnp.concatenate([c_ctx[None, :], c, jnp.zeros((ADA_ROWS - 1 - bs, D_MODEL), F32)], axis=0)
    m0, m1 = _ada_modulation(cond, w_ada_0, b_ada_0, w_ada_1, b_ada_1)
    split = lambda m, lo, hi: m[lo:hi].reshape(hi - lo, 3, D_MODEL)
    mod0_p, mod1_p = split(m0, 0, 1), split(m1, 0, 1)
    mod0_s, mod1_s = split(m0, 1, 1 + bs), split(m1, 1, 1 + bs)

    head_id = jnp.arange(D_ATT) // HEAD_DIM
    bd = (head_id[:, None] == head_id[None, :]).astype(BF16)
    common = (norm_g_0.reshape(1, D_MODEL), w_in_0.astype(BF16), conv_w_0, conv_b_0, conv_ln_g_0,
              conv_ln_b_0, jnp.tile(q_norm_0, NA_HEADS).reshape(1, D_ATT),
              jnp.tile(k_norm_0, NA_HEADS).reshape(1, D_ATT), bd, w_out_0.astype(BF16),
              norm_g_1.reshape(1, D_MODEL), w_in_1.astype(BF16), pool_w_1.astype(BF16), pool_scale_1,
              w_out_1.astype(BF16))

    y_prompt, k_ctx, v_ctx = _path(x_prompt, mod0_p, mod1_p, 256, lp, lp, common + (F32,),
                                   _context_attention)

    bias = _window_bias(rpb_0)
    kc3 = cache_k_0.reshape(bs, past, D_ATT)
    vc3 = cache_v_0.reshape(bs, past, D_ATT)
    na = lambda q3, k3, v3, sgb3: _neighbourhood_attention(q3, k3, v3, kc3, vc3, bias, sgb3, 8)
    y_sample, _, _ = _path(x_sample, mod0_s, mod1_s, 256, 512, 512, common + (BF16,), na)

    return (y_prompt, y_sample,
            k_ctx.reshape(bp, lp, NA_HEADS, HEAD_DIM), v_ctx.reshape(bp, lp, NA_HEADS, HEAD_DIM))
```

```python
import functools

import jax
import jax.numpy as jnp
from jax import lax
from jax.experimental import pallas as pl
from jax.experimental.pallas import tpu as pltpu

F32 = jnp.float32
BF16 = jnp.bfloat16

D_MODEL = 1024
EPS = 1e-6
D_CONV = 512
CONV_K = 31
CONV_HALO = 16
NA_HEADS = 8
HEAD_DIM = 64
D_ATT = NA_HEADS * HEAD_DIM
HEADS_PER_VREG = 2
N_HEAD_PAIRS = NA_HEADS // HEADS_PER_VREG
PAIR_W = HEADS_PER_VREG * HEAD_DIM
GRID_W = 64
NA_ROWS = 8
NA_COLS = 16
ATT_SCALE = HEAD_DIM ** -0.5
D_IN_EVEN = 3 * D_CONV + 4 * D_ATT
POOL_WINDOWS = (2, 4, 8, 16)
POOL_GD = D_MODEL // len(POOL_WINDOWS)
TAIL_HALO = 16
MASK_BIAS = -1e30
SUBLANES = 8
ADA_ROWS = SUBLANES
VMEM_LIMIT = 48 * 1024 * 1024


def _sigmoid(x):
    return 1.0 / (1.0 + jnp.exp(-x))


def _silu(x):
    return x * _sigmoid(x)


def _cparams(n_axes):
    return pltpu.CompilerParams(dimension_semantics=("parallel",) * n_axes,
                                vmem_limit_bytes=VMEM_LIMIT)


def _ada_kernel(cond_ref, w0_ref, b0_ref, w1_ref, b1_ref, m0_ref, m1_ref):
    s = _silu(cond_ref[...])
    m0_ref[...] = jnp.dot(s, w0_ref[...], preferred_element_type=F32,
                          precision=lax.Precision.HIGHEST) + b0_ref[...]
    m1_ref[...] = jnp.dot(s, w1_ref[...], preferred_element_type=F32,
                          precision=lax.Precision.HIGHEST) + b1_ref[...]


def _ada_modulation(cond, w_ada_0, b_ada_0, w_ada_1, b_ada_1):
    tn = 512
    n3 = 3 * D_MODEL
    wspec = pl.BlockSpec((D_MODEL, tn), lambda j: (0, j))
    bspec = pl.BlockSpec((1, tn), lambda j: (0, j))
    ospec = pl.BlockSpec((ADA_ROWS, tn), lambda j: (0, j))
    return pl.pallas_call(
        _ada_kernel,
        out_shape=(jax.ShapeDtypeStruct((ADA_ROWS, n3), F32),) * 2,
        grid=(n3 // tn,),
        in_specs=[pl.BlockSpec((ADA_ROWS, D_MODEL), lambda j: (0, 0)), wspec, bspec, wspec, bspec],
        out_specs=(ospec, ospec),
        compiler_params=_cparams(1),
        name="ada_modulation",
    )(cond, w_ada_0, b_ada_0.reshape(1, n3), w_ada_1, b_ada_1.reshape(1, n3))


def _mod_spec(mod, tiles_per_seq):
    n_mod = mod.shape[0]
    return pl.BlockSpec((1, 3, D_MODEL),
                        lambda i: (jnp.minimum(i // tiles_per_seq, n_mod - 1), 0, 0))


def _modulated_norm(x, g_row, mod_ref):
    ms = jnp.mean(x * x, axis=-1, keepdims=True)
    y = (x * lax.rsqrt(ms + EPS)) * g_row
    return y * (1.0 + mod_ref[0, 1:2, :]) + mod_ref[0, 0:1, :]


def _in0_kernel(x_ref, mod_ref, g_ref, w_ref, qn_ref, kn_ref, bd_ref,
                u_ref, sga_ref, q_ref, k_ref, v_ref, sgb_ref):
    hb = _modulated_norm(x_ref[...], g_ref[...], mod_ref).astype(BF16)

    def proj(c):
        return jnp.dot(hb, w_ref[:, c * D_CONV:(c + 1) * D_CONV], preferred_element_type=F32)

    def head_rms(t, g_row):
        ms = jnp.dot((t * t).astype(BF16), bd_ref[...], preferred_element_type=F32) * (1.0 / HEAD_DIM)
        return (t * lax.rsqrt(ms + EPS)) * g_row

    a = proj(0)
    u_ref[...] = a * _sigmoid(proj(1))
    sga_ref[...] = _silu(proj(2))
    q_ref[...] = (head_rms(proj(3), qn_ref[...]) * ATT_SCALE).astype(q_ref.dtype)
    k_ref[...] = head_rms(proj(4), kn_ref[...]).astype(k_ref.dtype)
    v_ref[...] = proj(5).astype(v_ref.dtype)
    sgb_ref[...] = _silu(proj(6))


def _layer0_input(x2, mod, L, tm, g0, w_in, qn, kn, bd, kv_dtype):
    n = x2.shape[0]
    tiles_per_seq = L // tm
    row = lambda i: (i, 0)
    const = lambda i: (0, 0)
    half = pl.BlockSpec((tm, D_CONV), row)
    return pl.pallas_call(
        _in0_kernel,
        out_shape=(jax.ShapeDtypeStruct((n, D_CONV), F32),
                   jax.ShapeDtypeStruct((n, D_CONV), F32),
                   jax.ShapeDtypeStruct((n, D_ATT), BF16),
                   jax.ShapeDtypeStruct((n, D_ATT), kv_dtype),
                   jax.ShapeDtypeStruct((n, D_ATT), kv_dtype),
                   jax.ShapeDtypeStruct((n, D_ATT), F32)),
        grid=(n // tm,),
        in_specs=[pl.BlockSpec((tm, D_MODEL), row),
                  _mod_spec(mod, tiles_per_seq),
                  pl.BlockSpec((1, D_MODEL), const),
                  pl.BlockSpec((D_MODEL, D_IN_EVEN), const),
                  pl.BlockSpec((1, D_ATT), const),
                  pl.BlockSpec((1, D_ATT), const),
                  pl.BlockSpec((D_ATT, D_ATT), const)],
        out_specs=(half,) * 6,
        compiler_params=_cparams(1),
        name="layer0_input",
    )(x2, mod, g0, w_in, qn, kn, bd)


def _conv_kernel(up_ref, u_ref, un_ref, sga_ref, w_ref, cb_ref, lg_ref, lb_ref, z_ref, buf_ref, sh_ref,
                 *, tile, n_tiles, chunk):
    i = pl.program_id(1)
    h = CONV_HALO
    buf_ref[0:h, :] = jnp.where(i > 0, up_ref[0], 0.0)
    buf_ref[h:h + tile, :] = u_ref[0]
    buf_ref[h + tile:2 * h + tile, :] = jnp.where(i < n_tiles - 1, un_ref[0], 0.0)
    off = h - CONV_K // 2
    span = sh_ref.shape[1]
    for s in range(SUBLANES):
        sh_ref[s] = buf_ref[s:s + span, :]

    for c in range(tile // chunk):
        base = c * chunk
        acc = jnp.zeros((chunk, D_CONV), F32)
        for k in range(CONV_K):
            s, a = (off + k) % SUBLANES, (off + k) // SUBLANES * SUBLANES
            acc = acc + sh_ref[s, base + a:base + a + chunk, :] * w_ref[k:k + 1, :]
        y = acc + cb_ref[...]
        mu = jnp.mean(y, axis=-1, keepdims=True)
        yc = y - mu
        var = jnp.mean(yc * yc, axis=-1, keepdims=True)
        yn = (yc * lax.rsqrt(var + EPS)) * lg_ref[...] + lb_ref[...]
        z = _silu(yn) * sga_ref[0, base:base + chunk, :]
        z_ref[0, base:base + chunk, :] = z.astype(z_ref.dtype)


def _halo_specs(tile, halo, seq, width):
    r = tile // halo
    last = seq // halo - 1
    prev = pl.BlockSpec((1, halo, width), lambda b, i: (b, jnp.maximum(i * r - 1, 0), 0))
    nxt = pl.BlockSpec((1, halo, width), lambda b, i: (b, jnp.minimum((i + 1) * r, last), 0))
    return prev, nxt


def _conv_module(u3, sga3, tile, conv_w, conv_b, ln_g, ln_b):
    bsz, L, _ = u3.shape
    n_tiles = L // tile
    main = pl.BlockSpec((1, tile, D_CONV), lambda b, i: (b, i, 0))
    prev, nxt = _halo_specs(tile, CONV_HALO, L, D_CONV)
    vec = pl.BlockSpec((1, D_CONV), lambda b, i: (0, 0))
    return pl.pallas_call(
        functools.partial(_conv_kernel, tile=tile, n_tiles=n_tiles, chunk=32),
        out_shape=jax.ShapeDtypeStruct((bsz, L, D_CONV), BF16),
        grid=(bsz, n_tiles),
        in_specs=[prev, main, nxt, main,
                  pl.BlockSpec((CONV_K, D_CONV), lambda b, i: (0, 0)), vec, vec, vec],
        out_specs=main,
        scratch_shapes=[pltpu.VMEM((tile + 2 * CONV_HALO, D_CONV), F32),
                        pltpu.VMEM((SUBLANES, tile + 2 * CONV_HALO - SUBLANES, D_CONV), F32)],
        compiler_params=_cparams(2),
        name="conv_module",
    )(u3, u3, u3, sga3, conv_w, conv_b.reshape(1, D_CONV), ln_g.reshape(1, D_CONV),
      ln_b.reshape(1, D_CONV))


def _dot_nt(a, b):
    return lax.dot_general(a, b, (((1,), (1,)), ((), ())), preferred_element_type=F32)


def _head_masks():
    lane = lax.broadcasted_iota(jnp.int32, (1, PAIR_W), 1)
    first = lane < HEAD_DIM
    return (first, jnp.logical_not(first))


def _keep(mask, t):
    return jnp.where(mask, t, jnp.zeros_like(t))


def _ctx_attn_kernel(q_ref, k_ref, v_ref, sgb_ref, z_ref):
    masks = _head_masks()

    def cols(p):
        return slice(p * PAIR_W, (p + 1) * PAIR_W)

    def scores(p, e):
        return _dot_nt(_keep(masks[e], q_ref[0, :, cols(p)]), k_ref[0, :, cols(p)].astype(BF16))

    def finish(p, e, s):
        pr = jnp.exp(s - jnp.max(s, axis=-1, keepdims=True))
        l = jnp.sum(pr, axis=-1, keepdims=True)
        vb = _keep(masks[e], v_ref[0, :, cols(p)].astype(BF16))
        return jnp.dot(pr.astype(BF16), vb, preferred_element_type=F32) / l

    chains = [(p, e) for p in range(N_HEAD_PAIRS) for e in range(HEADS_PER_VREG)]
    done = {}
    s_next = scores(*chains[0])
    for c, (p, e) in enumerate(chains):
        s_cur = s_next
        if c + 1 < len(chains):
            s_next = scores(*chains[c + 1])
        done[(p, e)] = finish(p, e, s_cur)
    for p in range(N_HEAD_PAIRS):
        o = done[(p, 0)] + done[(p, 1)]
        z_ref[0, :, cols(p)] = (o * sgb_ref[0, :, cols(p)]).astype(z_ref.dtype)


def _context_attention(q3, k3, v3, sgb3):
    bsz, L, _ = q3.shape
    blk = pl.BlockSpec((1, L, D_ATT), lambda b: (b, 0, 0))
    return pl.pallas_call(
        _ctx_attn_kernel,
        out_shape=jax.ShapeDtypeStruct((bsz, L, D_ATT), BF16),
        grid=(bsz,),
        in_specs=[blk, blk, blk, blk],
        out_specs=blk,
        compiler_params=_cparams(1),
        name="context_attention",
    )(q3, k3, v3, sgb3)


def _na_kernel(q_ref, k_ref, v_ref, kc_ref, vc_ref, bias_ref, sgb_ref, z_ref, *, rows_per_step, n_rows):
    rt = pl.program_id(2)
    masks = _head_masks()
    kc = kc_ref[0].astype(BF16)
    vc = vc_ref[0].astype(BF16)
    win = NA_ROWS * GRID_W
    stack = HEADS_PER_VREG * GRID_W
    q_stack = jnp.concatenate(
        [_keep(hm, q_ref[0, j * GRID_W:(j + 1) * GRID_W, :])
         for j in range(rows_per_step) for hm in masks], axis=0)
    s_ctx = _dot_nt(q_stack, kc)

    def window(j):
        r = rt * rows_per_step + j
        r_start = jnp.clip(r - NA_ROWS // 2, 0, n_rows - NA_ROWS)
        ws = pl.ds(pl.multiple_of(r_start * GRID_W, GRID_W), win)
        return r - r_start, ws

    def scores(j):
        r_off, ws = window(j)
        return _dot_nt(q_stack[j * stack:(j + 1) * stack], k_ref[0, ws, :]) + bias_ref[r_off, 0]

    def finish(j, s_win):
        _, ws = window(j)
        sc = s_ctx[j * stack:(j + 1) * stack]
        m = jnp.maximum(jnp.max(s_win, axis=-1, keepdims=True), jnp.max(sc, axis=-1, keepdims=True))
        p_win = jnp.exp(s_win - m)
        p_ctx = jnp.exp(sc - m)
        l = jnp.sum(p_win, axis=-1, keepdims=True) + jnp.sum(p_ctx, axis=-1, keepdims=True)
        o_win = jnp.dot(p_win.astype(BF16), v_ref[0, ws, :], preferred_element_type=F32)
        return o_win, p_ctx.astype(BF16), l

    done = []
    s_next = scores(0)
    for j in range(rows_per_step):
        s_cur = s_next
        if j + 1 < rows_per_step:
            s_next = scores(j + 1)
        done.append(finish(j, s_cur))

    o_win = jnp.concatenate([t[0] for t in done], axis=0)
    p_ctx = jnp.concatenate([t[1] for t in done], axis=0)
    l = jnp.concatenate([t[2] for t in done], axis=0)
    o = (o_win + jnp.dot(p_ctx, vc, preferred_element_type=F32)) / l
    for j in range(rows_per_step):
        oj = jnp.where(masks[0], o[j * stack:j * stack + GRID_W], o[j * stack + GRID_W:(j + 1) * stack])
        rows = slice(j * GRID_W, (j + 1) * GRID_W)
        z_ref[0, rows, :] = (oj * sgb_ref[0, rows, :]).astype(z_ref.dtype)


def _window_bias(rpb):
    qc = jnp.arange(GRID_W)
    kcol = jnp.arange(GRID_W)
    start = jnp.clip(qc - NA_COLS // 2, 0, GRID_W - NA_COLS)
    ok = (kcol[None, :] >= start[:, None]) & (kcol[None, :] < start[:, None] + NA_COLS)
    rel_c = kcol[None, :] - qc[:, None] + NA_COLS - 1
    r_off = jnp.arange(NA_ROWS)
    rel_r = jnp.arange(NA_ROWS)[None, :] - r_off[:, None] + NA_ROWS - 1
    onehot = ((rel_c[None] == jnp.arange(2 * NA_COLS - 1)[:, None, None]) & ok[None]).astype(F32)
    rows = rpb.astype(F32)[:, rel_r]
    b = jnp.einsum('hrjd,dqk->rhqjk', rows, onehot, precision=lax.Precision.HIGHEST)
    b = jnp.where(ok[None, None, :, None, :], b, MASK_BIAS)
    return b.reshape(NA_ROWS, N_HEAD_PAIRS, HEADS_PER_VREG * GRID_W, NA_ROWS * GRID_W)


def _neighbourhood_attention(q3, k3, v3, kc3, vc3, bias, sgb3, rows_per_step):
    bsz, L, _ = q3.shape
    n_rows = L // GRID_W
    past = kc3.shape[1]
    tq = rows_per_step * GRID_W
    qblk = pl.BlockSpec((1, tq, PAIR_W), lambda b, p, r: (b, r, p))
    full = pl.BlockSpec((1, L, PAIR_W), lambda b, p, r: (b, 0, p))
    ctx = pl.BlockSpec((1, past, PAIR_W), lambda b, p, r: (b, 0, p))
    bias_spec = pl.BlockSpec((NA_ROWS, 1, HEADS_PER_VREG * GRID_W, NA_ROWS * GRID_W),
                             lambda b, p, r: (0, p, 0, 0))
    return pl.pallas_call(
        functools.partial(_na_kernel, rows_per_step=rows_per_step, n_rows=n_rows),
        out_shape=jax.ShapeDtypeStruct((bsz, L, D_ATT), BF16),
        grid=(bsz, N_HEAD_PAIRS, n_rows // rows_per_step),
        in_specs=[qblk, full, full, ctx, ctx, bias_spec, qblk],
        out_specs=qblk,
        compiler_params=_cparams(3),
        name="neighbourhood_attention",
    )(q3, k3, v3, kc3, vc3, bias, sgb3)


def _shift_up(x, k):
    n = x.shape[0]
    return pltpu.roll(x, n - k, 0)


def _pool_window_sums(u, w):
    c = u
    m = 1
    while 2 * m < w:
        c = c + _shift_up(c, m)
        m *= 2
    return pltpu.roll(c, w // 2, 0) + c


def _tail_kernel(*refs, tile, n_tiles, seq):
    if n_tiles > 1:
        (xp_ref, x_ref, xn_ref, cp_ref, zc_ref, cn_ref, ap_ref, za_ref, an_ref,
         mod0_ref, mod1_ref, g1_ref, wo0_ref, w1_ref, pw_ref, ps_ref, wo1_ref, o_ref) = refs
        ext = lambda p, m, n: jnp.concatenate([p[0], m[0], n[0]], axis=0)
        x, zc, za = ext(xp_ref, x_ref, xn_ref), ext(cp_ref, zc_ref, cn_ref), ext(ap_ref, za_ref, an_ref)
    else:
        (x_ref, zc_ref, za_ref,
         mod0_ref, mod1_ref, g1_ref, wo0_ref, w1_ref, pw_ref, ps_ref, wo1_ref, o_ref) = refs
        x, zc, za = x_ref[0], zc_ref[0], za_ref[0]
    i = pl.program_id(1)
    h = TAIL_HALO
    out0 = (jnp.dot(zc, wo0_ref[0:D_CONV, :], preferred_element_type=F32)
            + jnp.dot(za, wo0_ref[D_CONV:D_CONV + D_ATT, :], preferred_element_type=F32))
    y = x + mod0_ref[0, 2:3, :] * out0
    hb = _modulated_norm(y, g1_ref[...], mod1_ref).astype(BF16)
    u = jnp.dot(hb, w1_ref[:, 0:D_MODEL], preferred_element_type=F32)
    if n_tiles > 1:
        row = lax.broadcasted_iota(jnp.int32, (tile + 2 * h, 1), 0)
        outside = ((row < h) & (i == 0)) | ((row >= h + tile) & (i == n_tiles - 1))
        u = jnp.where(outside, 0.0, u)
        y = y[h:h + tile]
        hb = hb[h:h + tile]
    else:
        pad = jnp.zeros((h, D_MODEL), F32)
        u = jnp.concatenate([pad, u, pad], axis=0)
    sg = _silu(jnp.dot(hb, w1_ref[:, D_MODEL:2 * D_MODEL], preferred_element_type=F32))
    t = (i * tile).astype(F32) + lax.broadcasted_iota(jnp.int32, (tile, 1), 0).astype(F32)
    out1 = jnp.zeros((tile, D_MODEL), F32)
    for g, w in enumerate(POOL_WINDOWS):
        cols = slice(g * POOL_GD, (g + 1) * POOL_GD)
        ug = u[:, cols]
        s = _pool_window_sums(ug, w)[h:h + tile]
        cnt = jnp.minimum(t + (w - w // 2), float(seq)) - jnp.maximum(t - w // 2, 0.0)
        d = s / cnt - ug[h:h + tile]
        yp = jnp.dot(d.astype(BF16), pw_ref[g], preferred_element_type=F32) * ps_ref[:, cols]
        z = (yp * sg[:, cols]).astype(BF16)
        out1 = out1 + jnp.dot(z, wo1_ref[cols, :], preferred_element_type=F32)
    o_ref[0] = y + mod1_ref[0, 2:3, :] * out1


def _layers_tail(x3, zc3, za3, mod0, mod1, tile, g1, w_out0, w_in1, pool_w, pool_scale, w_out1):
    bsz, L, _ = x3.shape
    n_tiles = L // tile
    n_mod = mod0.shape[0]
    const2 = lambda b, i: (0, 0)
    modspec = pl.BlockSpec((1, 3, D_MODEL), lambda b, i: (jnp.minimum(b, n_mod - 1), 0, 0))

    def tiled(width):
        main = pl.BlockSpec((1, tile, width), lambda b, i: (b, i, 0))
        if n_tiles == 1:
            return [main]
        prev, nxt = _halo_specs(tile, TAIL_HALO, L, width)
        return [prev, main, nxt]

    def operands(t):
        return [t] if n_tiles == 1 else [t, t, t]

    return pl.pallas_call(
        functools.partial(_tail_kernel, tile=tile, n_tiles=n_tiles, seq=L),
        out_shape=jax.ShapeDtypeStruct((bsz, L, D_MODEL), F32),
        grid=(bsz, n_tiles),
        in_specs=tiled(D_MODEL) + tiled(D_CONV) + tiled(D_ATT) + [
            modspec, modspec,
            pl.BlockSpec((1, D_MODEL), const2),
            pl.BlockSpec((D_CONV + D_ATT, D_MODEL), const2),
            pl.BlockSpec((D_MODEL, 2 * D_MODEL), const2),
            pl.BlockSpec((len(POOL_WINDOWS), POOL_GD, POOL_GD), lambda b, i: (0, 0, 0)),
            pl.BlockSpec((1, D_MODEL), const2),
            pl.BlockSpec((D_MODEL, D_MODEL), const2)],
        out_specs=pl.BlockSpec((1, tile, D_MODEL), lambda b, i: (b, i, 0)),
        compiler_params=_cparams(2),
        name="layers_tail",
    )(*operands(x3), *operands(zc3), *operands(za3), mod0, mod1, g1, w_out0, w_in1, pool_w,
      pool_scale.reshape(1, D_MODEL), w_out1)


def _path(x, mod0, mod1, tm, conv_tile, pool_tile, params, attention):
    (g0, w_in0, conv_w, conv_b, ln_g, ln_b, qn, kn, bd, w_out0, g1, w_in1, pool_w, pool_scale,
     w_out1, kv_dtype) = params
    bsz, L, _ = x.shape
    x2 = x.reshape(bsz * L, D_MODEL)
    u, sga, q, k, v, sgb = _layer0_input(x2, mod0, L, tm, g0, w_in0, qn, kn, bd, kv_dtype)
    r3 = lambda t: t.reshape(bsz, L, t.shape[-1])
    zc = _conv_module(r3(u), r3(sga), conv_tile, conv_w, conv_b, ln_g, ln_b)
    za = attention(r3(q), r3(k), r3(v), r3(sgb))
    out = _layers_tail(x, zc, za, mod0, mod1, pool_tile, g1, w_out0, w_in1, pool_w, pool_scale, w_out1)
    return out, k, v


def kernel(x_prompt, x_sample, cache_k_0, cache_v_0, c, c_ctx, norm_g_0, w_ada_0, b_ada_0, w_in_0, conv_w_0, conv_b_0, conv_ln_g_0, conv_ln_b_0, q_norm_0, k_norm_0, rpb_0, w_out_0, norm_g_1, w_ada_1, b_ada_1, w_in_1, pool_w_1, pool_scale_1, w_out_1):
    bp, lp, _ = x_prompt.shape
    bs, ls, _ = x_sample.shape
    past = cache_k_0.shape[1]

    cond = jnp.concatenate([c_ctx[None, :], c, jnp.zeros((ADA_ROWS - 1 - bs, D_MODEL), F32)], axis=0)
    m0, m1 = _ada_modulation(cond, w_ada_0, b_ada_0, w_ada_1, b_ada_1)
    split = lambda m, lo, hi: m[lo:hi].reshape(hi - lo, 3, D_MODEL)
    mod0_p, mod1_p = split(m0, 0, 1), split(m1, 0, 1)
    mod0_s, mod1_s = split(m0, 1, 1 + bs), split(m1, 1, 1 + bs)

    head_id = jnp.arange(D_ATT) // HEAD_DIM
    bd = (head_id[:, None] == head_id[None, :]).astype(BF16)
    common = (norm_g_0.reshape(1, D_MODEL), w_in_0.astype(BF16), conv_w_0, conv_b_0, conv_ln_g_0,
              conv_ln_b_0, jnp.tile(q_norm_0, NA_HEADS).reshape(1, D_ATT),
              jnp.tile(k_norm_0, NA_HEADS).reshape(1, D_ATT), bd, w_out_0.astype(BF16),
              norm_g_1.reshape(1, D_MODEL), w_in_1.astype(BF16), pool_w_1.astype(BF16), pool_scale_1,
              w_out_1.astype(BF16))

    y_prompt, k_ctx, v_ctx = _path(x_prompt, mod0_p, mod1_p, 256, lp, lp, common + (F32,),
                                   _context_attention)

    bias = _window_bias(rpb_0)
    kc3 = cache_k_0.reshape(bs, past, D_ATT)
    vc3 = cache_v_0.reshape(bs, past, D_ATT)
    na = lambda q3, k3, v3, sgb3: _neighbourhood_attention(q3, k3, v3, kc3, vc3, bias, sgb3, 8)
    y_sample, _, _ = _path(x_sample, mod0_s, mod1_s, 256, 512, 512, common + (BF16,), na)

    return (y_prompt, y_sample,
            k_ctx.reshape(bp, lp, NA_HEADS, HEAD_DIM), v_ctx.reshape(bp, lp, NA_HEADS, HEAD_DIM))
```

```python
import functools

import jax
import jax.numpy as jnp
from jax import lax
from jax.experimental import pallas as pl
from jax.experimental.pallas import tpu as pltpu

F32 = jnp.float32
BF16 = jnp.bfloat16

D_MODEL = 1024
EPS = 1e-6
D_CONV = 512
CONV_K = 31
CONV_HALO = 16
NA_HEADS = 8
HEAD_DIM = 64
D_ATT = NA_HEADS * HEAD_DIM
HEADS_PER_VREG = 2
N_HEAD_PAIRS = NA_HEADS // HEADS_PER_VREG
PAIR_W = HEADS_PER_VREG * HEAD_DIM
GRID_W = 64
NA_ROWS = 8
NA_COLS = 16
ATT_SCALE = HEAD_DIM ** -0.5
D_IN_EVEN = 3 * D_CONV + 4 * D_ATT
POOL_WINDOWS = (2, 4, 8, 16)
POOL_GD = D_MODEL // len(POOL_WINDOWS)
TAIL_HALO = 16
MASK_BIAS = -1e30
SUBLANES = 8
ADA_ROWS = SUBLANES
VMEM_LIMIT = 48 * 1024 * 1024


def _sigmoid(x):
    return 1.0 / (1.0 + jnp.exp(-x))


def _silu(x):
    return x * _sigmoid(x)


def _cparams(n_axes, flags=None):
    return pltpu.CompilerParams(dimension_semantics=("parallel",) * n_axes,
                                vmem_limit_bytes=VMEM_LIMIT, flags=flags)


def _ada_kernel(cond_ref, w0_ref, b0_ref, w1_ref, b1_ref, m0_ref, m1_ref):
    s = _silu(cond_ref[...])
    m0_ref[...] = jnp.dot(s, w0_ref[...], preferred_element_type=F32,
                          precision=lax.Precision.HIGHEST) + b0_ref[...]
    m1_ref[...] = jnp.dot(s, w1_ref[...], preferred_element_type=F32,
                          precision=lax.Precision.HIGHEST) + b1_ref[...]


def _ada_modulation(cond, w_ada_0, b_ada_0, w_ada_1, b_ada_1):
    tn = 512
    n3 = 3 * D_MODEL
    wspec = pl.BlockSpec((D_MODEL, tn), lambda j: (0, j))
    bspec = pl.BlockSpec((1, tn), lambda j: (0, j))
    ospec = pl.BlockSpec((ADA_ROWS, tn), lambda j: (0, j))
    return pl.pallas_call(
        _ada_kernel,
        out_shape=(jax.ShapeDtypeStruct((ADA_ROWS, n3), F32),) * 2,
        grid=(n3 // tn,),
        in_specs=[pl.BlockSpec((ADA_ROWS, D_MODEL), lambda j: (0, 0)), wspec, bspec, wspec, bspec],
        out_specs=(ospec, ospec),
        compiler_params=_cparams(1),
        name="ada_modulation",
    )(cond, w_ada_0, b_ada_0.reshape(1, n3), w_ada_1, b_ada_1.reshape(1, n3))


def _modulated_norm(x, g_row, mod_ref):
    ms = jnp.mean(x * x, axis=-1, keepdims=True)
    y = (x * lax.rsqrt(ms + EPS)) * g_row
    return y * (1.0 + mod_ref[0, 1:2, :]) + mod_ref[0, 0:1, :]


def _head_kernel(*refs, tile, n_tiles, chunk):
    if n_tiles > 1:
        xp_ref, x_ref, xn_ref = refs[:3]
        refs = refs[3:]
        x = jnp.concatenate([xp_ref[0], x_ref[0], xn_ref[0]], axis=0)
    else:
        x_ref = refs[0]
        refs = refs[1:]
        x = x_ref[0]
    (mod_ref, g_ref, w_ref, qn_ref, kn_ref, bd_ref, cw_ref, cb_ref, lg_ref, lb_ref,
     zc_ref, q_ref, k_ref, v_ref, sgb_ref, buf_ref, sh_ref, sga_ref, hb_ref, p_ref) = refs
    i = pl.program_id(1)
    h = CONV_HALO
    hb = _modulated_norm(x, g_ref[...], mod_ref).astype(BF16)

    def proj(t, c):
        return jnp.dot(t, w_ref[:, c * D_CONV:(c + 1) * D_CONV], preferred_element_type=F32)

    def head_rms(t, g_row):
        ms = jnp.dot((t * t).astype(BF16), bd_ref[...], preferred_element_type=F32) * (1.0 / HEAD_DIM)
        return (t * lax.rsqrt(ms + EPS)) * g_row

    u = proj(hb, 0) * _sigmoid(proj(hb, 1))
    if n_tiles > 1:
        row = lax.broadcasted_iota(jnp.int32, (tile + 2 * h, 1), 0)
        outside = ((row < h) & (i == 0)) | ((row >= h + tile) & (i == n_tiles - 1))
        buf_ref[...] = jnp.where(outside, 0.0, u)
        hb = hb[h:h + tile]
    else:
        pad = jnp.zeros((h, D_CONV), F32)
        buf_ref[0:h, :] = pad
        buf_ref[h:h + tile, :] = u
        buf_ref[h + tile:2 * h + tile, :] = pad
    span = sh_ref.shape[1]
    for s in range(SUBLANES):
        sh_ref[s] = buf_ref[s:s + span, :]
    sga_ref[...] = _silu(proj(hb, 2))
    hb_ref[...] = hb
    off = h - CONV_K // 2
    n_chunks = tile // chunk
    rest = D_IN_EVEN - 3 * D_CONV
    piece = rest // n_chunks

    never = lax.broadcasted_iota(jnp.int32, (chunk, D_CONV), 0) < jnp.minimum(pl.program_id(0), 0)
    for c in range(n_chunks):
        base = c * chunk
        col = c * piece
        pp = jnp.dot(hb_ref[...], w_ref[:, 3 * D_CONV + col:3 * D_CONV + col + piece],
                     preferred_element_type=F32)
        p_ref[:, col:col + piece] = pp
        anchor = jnp.concatenate([pp[r * chunk:(r + 1) * chunk] for r in range(D_CONV // piece)], axis=1)
        acc = jnp.where(never, anchor, 0.0).reshape(chunk // SUBLANES, SUBLANES, D_CONV)
        for k in range(CONV_K):
            s, a = (off + k) % SUBLANES, (off + k) // SUBLANES * SUBLANES
            taps = sh_ref[s, base + a:base + a + chunk, :].reshape(chunk // SUBLANES, SUBLANES, D_CONV)
            acc = acc + taps * cw_ref[k][None]
        y = acc.reshape(chunk, D_CONV) + cb_ref[...]
        mu = jnp.mean(y, axis=-1, keepdims=True)
        yc = y - mu
        var = jnp.mean(yc * yc, axis=-1, keepdims=True)
        yn = (yc * lax.rsqrt(var + EPS)) * lg_ref[...] + lb_ref[...]
        z = _silu(yn) * sga_ref[base:base + chunk, :]
        zc_ref[0, base:base + chunk, :] = z.astype(zc_ref.dtype)

    q_ref[0] = (head_rms(p_ref[:, 0:D_ATT], qn_ref[...]) * ATT_SCALE).astype(q_ref.dtype)
    k_ref[0] = head_rms(p_ref[:, D_ATT:2 * D_ATT], kn_ref[...]).astype(k_ref.dtype)
    v_ref[0] = p_ref[:, 2 * D_ATT:3 * D_ATT].astype(v_ref.dtype)
    sgb_ref[0] = _silu(p_ref[:, 3 * D_ATT:4 * D_ATT])


def _layer0_head(x3, mod, tile, g0, w_in, qn, kn, bd, conv_w, conv_b, ln_g, ln_b, kv_dtype):
    bsz, L, _ = x3.shape
    n_tiles = L // tile
    n_mod = mod.shape[0]
    const2 = lambda b, i: (0, 0)
    main = lambda width: pl.BlockSpec((1, tile, width), lambda b, i: (b, i, 0))
    if n_tiles == 1:
        x_specs, x_ops = [main(D_MODEL)], [x3]
    else:
        prev, nxt = _halo_specs(tile, CONV_HALO, L, D_MODEL)
        x_specs, x_ops = [prev, main(D_MODEL), nxt], [x3, x3, x3]
    vec = pl.BlockSpec((1, D_CONV), const2)
    out = lambda dtype: jax.ShapeDtypeStruct((bsz, L, D_CONV), dtype)
    ext = tile + 2 * CONV_HALO
    return pl.pallas_call(
        functools.partial(_head_kernel, tile=tile, n_tiles=n_tiles, chunk=32),
        out_shape=(out(BF16),
                   out(BF16),
                   out(kv_dtype),
                   out(kv_dtype),
                   out(F32)),
        grid=(bsz, n_tiles),
        in_specs=x_specs + [
            pl.BlockSpec((1, 3, D_MODEL), lambda b, i: (jnp.minimum(b, n_mod - 1), 0, 0)),
            pl.BlockSpec((1, D_MODEL), const2),
            pl.BlockSpec((D_MODEL, D_IN_EVEN), const2),
            vec, vec,
            pl.BlockSpec((D_ATT, D_ATT), const2),
            pl.BlockSpec((CONV_K, SUBLANES, D_CONV), lambda b, i: (0, 0, 0)),
            vec, vec, vec],
        out_specs=(main(D_CONV),) * 5,
        scratch_shapes=[pltpu.VMEM((ext, D_CONV), F32),
                        pltpu.VMEM((SUBLANES, ext - SUBLANES, D_CONV), F32),
                        pltpu.VMEM((tile, D_CONV), F32),
                        pltpu.VMEM((tile, D_MODEL), BF16),
                        pltpu.VMEM((tile, D_IN_EVEN - 3 * D_CONV), F32)],
        compiler_params=_cparams(2),
        name="layer0_head",
    )(*x_ops, mod, g0, w_in, qn, kn, bd,
      jnp.broadcast_to(conv_w[:, None, :], (CONV_K, SUBLANES, D_CONV)),
      conv_b.reshape(1, D_CONV), ln_g.reshape(1, D_CONV), ln_b.reshape(1, D_CONV))


def _halo_specs(tile, halo, seq, width):
    r = tile // halo
    last = seq // halo - 1
    prev = pl.BlockSpec((1, halo, width), lambda b, i: (b, jnp.maximum(i * r - 1, 0), 0))
    nxt = pl.BlockSpec((1, halo, width), lambda b, i: (b, jnp.minimum((i + 1) * r, last), 0))
    return prev, nxt


def _dot_nt(a, b):
    return lax.dot_general(a, b, (((1,), (1,)), ((), ())), preferred_element_type=F32)


def _head_masks():
    lane = lax.broadcasted_iota(jnp.int32, (1, PAIR_W), 1)
    first = lane < HEAD_DIM
    return (first, jnp.logical_not(first))


def _keep(mask, t):
    return jnp.where(mask, t, jnp.zeros_like(t))


def _ctx_attn_kernel(q_ref, k_ref, v_ref, sgb_ref, z_ref):
    masks = _head_masks()

    def cols(p):
        return slice(p * PAIR_W, (p + 1) * PAIR_W)

    def scores(p, e):
        return _dot_nt(_keep(masks[e], q_ref[0, :, cols(p)]), k_ref[0, :, cols(p)].astype(BF16))

    def finish(p, e, s):
        pr = jnp.exp(s - jnp.max(s, axis=-1, keepdims=True))
        l = jnp.sum(pr, axis=-1, keepdims=True)
        vb = _keep(masks[e], v_ref[0, :, cols(p)].astype(BF16))
        return jnp.dot(pr.astype(BF16), vb, preferred_element_type=F32) / l

    chains = [(p, e) for p in range(N_HEAD_PAIRS) for e in range(HEADS_PER_VREG)]
    done = {}
    s_next = scores(*chains[0])
    for c, (p, e) in enumerate(chains):
        s_cur = s_next
        if c + 1 < len(chains):
            s_next = scores(*chains[c + 1])
        done[(p, e)] = finish(p, e, s_cur)
    for p in range(N_HEAD_PAIRS):
        o = done[(p, 0)] + done[(p, 1)]
        z_ref[0, :, cols(p)] = (o * sgb_ref[0, :, cols(p)]).astype(z_ref.dtype)


def _context_attention(q3, k3, v3, sgb3):
    bsz, L, _ = q3.shape
    blk = pl.BlockSpec((1, L, D_ATT), lambda b: (b, 0, 0))
    return pl.pallas_call(
        _ctx_attn_kernel,
        out_shape=jax.ShapeDtypeStruct((bsz, L, D_ATT), BF16),
        grid=(bsz,),
        in_specs=[blk, blk, blk, blk],
        out_specs=blk,
        compiler_params=_cparams(1),
        name="context_attention",
    )(q3, k3, v3, sgb3)


def _na_kernel(q_ref, k_ref, v_ref, kc_ref, vc_ref, bias_ref, sgb_ref, z_ref, *, rows_per_step, n_rows):
    rt = pl.program_id(2)
    masks = _head_masks()
    kc = kc_ref[0].astype(BF16)
    vc = vc_ref[0].astype(BF16)
    win = NA_ROWS * GRID_W
    stack = HEADS_PER_VREG * GRID_W
    q_stack = jnp.concatenate(
        [_keep(hm, q_ref[0, j * GRID_W:(j + 1) * GRID_W, :])
         for j in range(rows_per_step) for hm in masks], axis=0)
    s_ctx = _dot_nt(q_stack, kc)

    def window(j):
        r = rt * rows_per_step + j
        r_start = jnp.clip(r - NA_ROWS // 2, 0, n_rows - NA_ROWS)
        ws = pl.ds(pl.multiple_of(r_start * GRID_W, GRID_W), win)
        return r - r_start, ws

    def scores(j):
        r_off, ws = window(j)
        return _dot_nt(q_stack[j * stack:(j + 1) * stack], k_ref[0, ws, :]) + bias_ref[r_off, 0]

    def finish(j, s_win):
        _, ws = window(j)
        sc = s_ctx[j * stack:(j + 1) * stack]
        m = jnp.maximum(jnp.max(s_win, axis=-1, keepdims=True), jnp.max(sc, axis=-1, keepdims=True))
        p_win = jnp.exp(s_win - m)
        p_ctx = jnp.exp(sc - m)
        l = jnp.sum(p_win, axis=-1, keepdims=True) + jnp.sum(p_ctx, axis=-1, keepdims=True)
        o_win = jnp.dot(p_win.astype(BF16), v_ref[0, ws, :], preferred_element_type=F32)
        return o_win, p_ctx.astype(BF16), l

    done = []
    s_next = scores(0)
    for j in range(rows_per_step):
        s_cur = s_next
        if j + 1 < rows_per_step:
            s_next = scores(j + 1)
        done.append(finish(j, s_cur))

    o_win = jnp.concatenate([t[0] for t in done], axis=0)
    p_ctx = jnp.concatenate([t[1] for t in done], axis=0)
    l = jnp.concatenate([t[2] for t in done], axis=0)
    o = (o_win + jnp.dot(p_ctx, vc, preferred_element_type=F32)) / l
    for j in range(rows_per_step):
        oj = jnp.where(masks[0], o[j * stack:j * stack + GRID_W], o[j * stack + GRID_W:(j + 1) * stack])
        rows = slice(j * GRID_W, (j + 1) * GRID_W)
        z_ref[0, rows, :] = (oj * sgb_ref[0, rows, :]).astype(z_ref.dtype)


def _window_bias(rpb):
    qc = jnp.arange(GRID_W)
    kcol = jnp.arange(GRID_W)
    start = jnp.clip(qc - NA_COLS // 2, 0, GRID_W - NA_COLS)
    ok = (kcol[None, :] >= start[:, None]) & (kcol[None, :] < start[:, None] + NA_COLS)
    rel_c = kcol[None, :] - qc[:, None] + NA_COLS - 1
    r_off = jnp.arange(NA_ROWS)
    rel_r = jnp.arange(NA_ROWS)[None, :] - r_off[:, None] + NA_ROWS - 1
    onehot = ((rel_c[None] == jnp.arange(2 * NA_COLS - 1)[:, None, None]) & ok[None]).astype(F32)
    rows = rpb.astype(F32)[:, rel_r]
    b = jnp.einsum('hrjd,dqk->rhqjk', rows, onehot, precision=lax.Precision.HIGHEST)
    b = jnp.where(ok[None, None, :, None, :], b, MASK_BIAS)
    return b.reshape(NA_ROWS, N_HEAD_PAIRS, HEADS_PER_VREG * GRID_W, NA_ROWS * GRID_W)


def _neighbourhood_attention(q3, k3, v3, kc3, vc3, bias, sgb3, rows_per_step):
    bsz, L, _ = q3.shape
    n_rows = L // GRID_W
    past = kc3.shape[1]
    tq = rows_per_step * GRID_W
    qblk = pl.BlockSpec((1, tq, PAIR_W), lambda b, p, r: (b, r, p))
    full = pl.BlockSpec((1, L, PAIR_W), lambda b, p, r: (b, 0, p))
    ctx = pl.BlockSpec((1, past, PAIR_W), lambda b, p, r: (b, 0, p))
    bias_spec = pl.BlockSpec((NA_ROWS, 1, HEADS_PER_VREG * GRID_W, NA_ROWS * GRID_W),
                             lambda b, p, r: (0, p, 0, 0))
    return pl.pallas_call(
        functools.partial(_na_kernel, rows_per_step=rows_per_step, n_rows=n_rows),
        out_shape=jax.ShapeDtypeStruct((bsz, L, D_ATT), BF16),
        grid=(bsz, N_HEAD_PAIRS, n_rows // rows_per_step),
        in_specs=[qblk, full, full, ctx, ctx, bias_spec, qblk],
        out_specs=qblk,
        compiler_params=_cparams(3),
        name="neighbourhood_attention",
    )(q3, k3, v3, kc3, vc3, bias, sgb3)


def _shift_up(x, k):
    n = x.shape[0]
    return pltpu.roll(x, n - k, 0)


def _pool_window_sums(u, w):
    c = u
    m = 1
    while 2 * m < w:
        c = c + _shift_up(c, m)
        m *= 2
    return pltpu.roll(c, w // 2, 0) + c


def _tail_kernel(*refs, tile, n_tiles, seq):
    if n_tiles > 1:
        (xp_ref, x_ref, xn_ref, cp_ref, zc_ref, cn_ref, ap_ref, za_ref, an_ref,
         mod0_ref, mod1_ref, g1_ref, wo0_ref, w1_ref, pw_ref, ps_ref, wo1_ref, o_ref) = refs
        ext = lambda p, m, n: jnp.concatenate([p[0], m[0], n[0]], axis=0)
        x, zc, za = ext(xp_ref, x_ref, xn_ref), ext(cp_ref, zc_ref, cn_ref), ext(ap_ref, za_ref, an_ref)
    else:
        (x_ref, zc_ref, za_ref,
         mod0_ref, mod1_ref, g1_ref, wo0_ref, w1_ref, pw_ref, ps_ref, wo1_ref, o_ref) = refs
        x, zc, za = x_ref[0], zc_ref[0], za_ref[0]
    i = pl.program_id(1)
    h = TAIL_HALO
    out0 = (jnp.dot(zc, wo0_ref[0:D_CONV, :], preferred_element_type=F32)
            + jnp.dot(za, wo0_ref[D_CONV:D_CONV + D_ATT, :], preferred_element_type=F32))
    y = x + mod0_ref[0, 2:3, :] * out0
    hb = _modulated_norm(y, g1_ref[...], mod1_ref).astype(BF16)
    u = jnp.dot(hb, w1_ref[:, 0:D_MODEL], preferred_element_type=F32)
    if n_tiles > 1:
        row = lax.broadcasted_iota(jnp.int32, (tile + 2 * h, 1), 0)
        outside = ((row < h) & (i == 0)) | ((row >= h + tile) & (i == n_tiles - 1))
        u = jnp.where(outside, 0.0, u)
        y = y[h:h + tile]
        hb = hb[h:h + tile]
    else:
        pad = jnp.zeros((h, D_MODEL), F32)
        u = jnp.concatenate([pad, u, pad], axis=0)
    sg = _silu(jnp.dot(hb, w1_ref[:, D_MODEL:2 * D_MODEL], preferred_element_type=F32))
    t = (i * tile).astype(F32) + lax.broadcasted_iota(jnp.int32, (tile, 1), 0).astype(F32)
    out1 = jnp.zeros((tile, D_MODEL), F32)
    for g, w in enumerate(POOL_WINDOWS):
        cols = slice(g * POOL_GD, (g + 1) * POOL_GD)
        ug = u[:, cols]
        s = _pool_window_sums(ug, w)[h:h + tile]
        cnt = jnp.minimum(t + (w - w // 2), float(seq)) - jnp.maximum(t - w // 2, 0.0)
        d = s / cnt - ug[h:h + tile]
        yp = jnp.dot(d.astype(BF16), pw_ref[g], preferred_element_type=F32) * ps_ref[:, cols]
        z = (yp * sg[:, cols]).astype(BF16)
        out1 = out1 + jnp.dot(z, wo1_ref[cols, :], preferred_element_type=F32)
    o_ref[0] = y + mod1_ref[0, 2:3, :] * out1


def _layers_tail(x3, zc3, za3, mod0, mod1, tile, g1, w_out0, w_in1, pool_w, pool_scale, w_out1):
    bsz, L, _ = x3.shape
    n_tiles = L // tile
    n_mod = mod0.shape[0]
    const2 = lambda b, i: (0, 0)
    modspec = pl.BlockSpec((1, 3, D_MODEL), lambda b, i: (jnp.minimum(b, n_mod - 1), 0, 0))

    def tiled(width):
        main = pl.BlockSpec((1, tile, width), lambda b, i: (b, i, 0))
        if n_tiles == 1:
            return [main]
        prev, nxt = _halo_specs(tile, TAIL_HALO, L, width)
        return [prev, main, nxt]

    def operands(t):
        return [t] if n_tiles == 1 else [t, t, t]

    return pl.pallas_call(
        functools.partial(_tail_kernel, tile=tile, n_tiles=n_tiles, seq=L),
        out_shape=jax.ShapeDtypeStruct((bsz, L, D_MODEL), F32),
        grid=(bsz, n_tiles),
        in_specs=tiled(D_MODEL) + tiled(D_CONV) + tiled(D_ATT) + [
            modspec, modspec,
            pl.BlockSpec((1, D_MODEL), const2),
            pl.BlockSpec((D_CONV + D_ATT, D_MODEL), const2),
            pl.BlockSpec((D_MODEL, 2 * D_MODEL), const2),
            pl.BlockSpec((len(POOL_WINDOWS), POOL_GD, POOL_GD), lambda b, i: (0, 0, 0)),
            pl.BlockSpec((1, D_MODEL), const2),
            pl.BlockSpec((D_MODEL, D_MODEL), const2)],
        out_specs=pl.BlockSpec((1, tile, D_MODEL), lambda b, i: (b, i, 0)),
        compiler_params=_cparams(2),
        name="layers_tail",
    )(*operands(x3), *operands(zc3), *operands(za3), mod0, mod1, g1, w_out0, w_in1, pool_w,
      pool_scale.reshape(1, D_MODEL), w_out1)


def _path(x, mod0, mod1, head_tile, tail_tile, params, attention):
    (g0, w_in0, conv_w, conv_b, ln_g, ln_b, qn, kn, bd, w_out0, g1, w_in1, pool_w, pool_scale,
     w_out1, kv_dtype) = params
    zc, q, k, v, sgb = _layer0_head(x, mod0, head_tile, g0, w_in0, qn, kn, bd, conv_w, conv_b, ln_g, ln_b,
                                    kv_dtype)
    za = attention(q, k, v, sgb)
    out = _layers_tail(x, zc, za, mod0, mod1, tail_tile, g1, w_out0, w_in1, pool_w, pool_scale, w_out1)
    return out, k, v


def kernel(x_prompt, x_sample, cache_k_0, cache_v_0, c, c_ctx, norm_g_0, w_ada_0, b_ada_0, w_in_0, conv_w_0, conv_b_0, conv_ln_g_0, conv_ln_b_0, q_norm_0, k_norm_0, rpb_0, w_out_0, norm_g_1, w_ada_1, b_ada_1, w_in_1, pool_w_1, pool_scale_1, w_out_1):
    bp, lp, _ = x_prompt.shape
    bs, ls, _ = x_sample.shape
    past = cache_k_0.shape[1]

    cond = jnp.concatenate([c_ctx[None, :], c, jnp.zeros((ADA_ROWS - 1 - bs, D_MODEL), F32)], axis=0)
    m0, m1 = _ada_modulation(cond, w_ada_0, b_ada_0, w_ada_1, b_ada_1)
    split = lambda m, lo, hi: m[lo:hi].reshape(hi - lo, 3, D_MODEL)
    mod0_p, mod1_p = split(m0, 0, 1), split(m1, 0, 1)
    mod0_s, mod1_s = split(m0, 1, 1 + bs), split(m1, 1, 1 + bs)

    head_id = jnp.arange(D_ATT) // HEAD_DIM
    bd = (head_id[:, None] == head_id[None, :]).astype(BF16)
    common = (norm_g_0.reshape(1, D_MODEL), w_in_0.astype(BF16), conv_w_0, conv_b_0, conv_ln_g_0,
              conv_ln_b_0, jnp.tile(q_norm_0, NA_HEADS).reshape(1, D_ATT),
              jnp.tile(k_norm_0, NA_HEADS).reshape(1, D_ATT), bd, w_out_0.astype(BF16),
              norm_g_1.reshape(1, D_MODEL), w_in_1.astype(BF16), pool_w_1.astype(BF16), pool_scale_1,
              w_out_1.astype(BF16))

    y_prompt, k_ctx, v_ctx = _path(x_prompt, mod0_p, mod1_p, lp, lp, common + (F32,), _context_attention)

    bias = _window_bias(rpb_0)
    kc3 = cache_k_0.reshape(bs, past, D_ATT)
    vc3 = cache_v_0.reshape(bs, past, D_ATT)
    na = lambda q3, k3, v3, sgb3: _neighbourhood_attention(q3, k3, v3, kc3, vc3, bias, sgb3, 8)
    y_sample, _, _ = _path(x_sample, mod0_s, mod1_s, 256, 512, common + (BF16,), na)

    return (y_prompt, y_sample,
            k_ctx.reshape(bp, lp, NA_HEADS, HEAD_DIM), v_ctx.reshape(bp, lp, NA_HEADS, HEAD_DIM))
```

```python
import functools

import jax
import jax.numpy as jnp
from jax import lax
from jax.experimental import pallas as pl
from jax.experimental.pallas import tpu as pltpu

F32 = jnp.float32
BF16 = jnp.bfloat16

D_MODEL = 1024
EPS = 1e-6
D_CONV = 512
CONV_K = 31
CONV_HALO = 16
NA_HEADS = 8
HEAD_DIM = 64
D_ATT = NA_HEADS * HEAD_DIM
HEADS_PER_VREG = 2
N_HEAD_PAIRS = NA_HEADS // HEADS_PER_VREG
PAIR_W = HEADS_PER_VREG * HEAD_DIM
GRID_W = 64
NA_ROWS = 8
NA_COLS = 16
ATT_SCALE = HEAD_DIM ** -0.5
D_IN_EVEN = 3 * D_CONV + 4 * D_ATT
POOL_WINDOWS = (2, 4, 8, 16)
POOL_GD = D_MODEL // len(POOL_WINDOWS)
TAIL_HALO = 16
MASK_BIAS = -1e30
SUBLANES = 8
MXU_WIDTH = 256
ADA_ROWS = SUBLANES
VMEM_LIMIT = 48 * 1024 * 1024


def _sigmoid(x):
    return 1.0 / (1.0 + jnp.exp(-x))


def _silu(x):
    return x * _sigmoid(x)


def _resident(shape, index_map):
    return pl.BlockSpec(shape, index_map, pipeline_mode=pl.Buffered(1))


def _cparams(n_axes, flags=None):
    return pltpu.CompilerParams(dimension_semantics=("parallel",) * n_axes,
                                vmem_limit_bytes=VMEM_LIMIT, flags=flags)


def _ada_kernel(cond_ref, w0_ref, b0_ref, w1_ref, b1_ref, m0_ref, m1_ref):
    s = _silu(cond_ref[...])
    m0_ref[...] = jnp.dot(s, w0_ref[...], preferred_element_type=F32,
                          precision=lax.Precision.HIGHEST) + b0_ref[...]
    m1_ref[...] = jnp.dot(s, w1_ref[...], preferred_element_type=F32,
                          precision=lax.Precision.HIGHEST) + b1_ref[...]


def _ada_modulation(cond, w_ada_0, b_ada_0, w_ada_1, b_ada_1):
    tn = 512
    n3 = 3 * D_MODEL
    wspec = pl.BlockSpec((D_MODEL, tn), lambda j: (0, j))
    bspec = pl.BlockSpec((1, tn), lambda j: (0, j))
    ospec = pl.BlockSpec((ADA_ROWS, tn), lambda j: (0, j))
    return pl.pallas_call(
        _ada_kernel,
        out_shape=(jax.ShapeDtypeStruct((ADA_ROWS, n3), F32),) * 2,
        grid=(n3 // tn,),
        in_specs=[pl.BlockSpec((ADA_ROWS, D_MODEL), lambda j: (0, 0)), wspec, bspec, wspec, bspec],
        out_specs=(ospec, ospec),
        compiler_params=_cparams(1),
        name="ada_modulation",
    )(cond, w_ada_0, b_ada_0.reshape(1, n3), w_ada_1, b_ada_1.reshape(1, n3))


def _modulated_norm(x, g_row, mod_ref):
    ms = jnp.mean(x * x, axis=-1, keepdims=True)
    y = (x * lax.rsqrt(ms + EPS)) * g_row
    return y * (1.0 + mod_ref[0, 1:2, :]) + mod_ref[0, 0:1, :]


def _head_kernel(*refs, tile, n_tiles, chunk):
    if n_tiles > 1:
        xp_ref, x_ref, xn_ref = refs[:3]
        refs = refs[3:]
        x = jnp.concatenate([xp_ref[0], x_ref[0], xn_ref[0]], axis=0)
    else:
        x_ref = refs[0]
        refs = refs[1:]
        x = x_ref[0]
    (mod_ref, g_ref, w_ref, qn_ref, kn_ref, bd_ref, cw_ref, cb_ref, lg_ref, lb_ref,
     zc_ref, q_ref, k_ref, v_ref, sgb_ref, buf_ref, sh_ref, sga_ref, hb_ref, p_ref) = refs
    i = pl.program_id(1)
    h = CONV_HALO
    hb = _modulated_norm(x, g_ref[...], mod_ref).astype(BF16)

    def proj(t, c):
        return jnp.dot(t, w_ref[:, c * D_CONV:(c + 1) * D_CONV], preferred_element_type=F32)

    def head_rms(t, g_row):
        ms = jnp.dot((t * t).astype(BF16), bd_ref[...], preferred_element_type=F32) * (1.0 / HEAD_DIM)
        return (t * lax.rsqrt(ms + EPS)) * g_row

    u = proj(hb, 0) * _sigmoid(proj(hb, 1))
    if n_tiles > 1:
        row = lax.broadcasted_iota(jnp.int32, (tile + 2 * h, 1), 0)
        outside = ((row < h) & (i == 0)) | ((row >= h + tile) & (i == n_tiles - 1))
        buf_ref[...] = jnp.where(outside, 0.0, u)
        hb = hb[h:h + tile]
    else:
        pad = jnp.zeros((h, D_CONV), F32)
        buf_ref[0:h, :] = pad
        buf_ref[h:h + tile, :] = u
        buf_ref[h + tile:2 * h + tile, :] = pad
    span = tile + 2 * h - SUBLANES
    for s in range(SUBLANES):
        sh_ref[s, 0:span, :] = buf_ref[s:s + span, :]
    sga_ref[...] = _silu(proj(hb, 2))
    hb_ref[...] = hb
    off = h - CONV_K // 2
    n_chunks = tile // chunk
    rest = D_IN_EVEN - 3 * D_CONV
    piece = MXU_WIDTH
    chunks_per_piece = n_chunks * piece // rest

    never = lax.broadcasted_iota(jnp.int32, (chunk, D_CONV), 0) < jnp.minimum(pl.program_id(0), 0)
    for c in range(n_chunks):
        base = c * chunk
        if c % chunks_per_piece == 0:
            col = c // chunks_per_piece * piece
            pp = jnp.dot(hb_ref[...], w_ref[:, 3 * D_CONV + col:3 * D_CONV + col + piece],
                         preferred_element_type=F32)
            p_ref[:, col:col + piece] = pp
        anchor = jnp.concatenate([pp[r * chunk:(r + 1) * chunk] for r in range(D_CONV // piece)], axis=1)
        acc = jnp.where(never, anchor, 0.0).reshape(chunk // SUBLANES, SUBLANES, D_CONV)
        for k in range(CONV_K):
            s, a = (off + k) % SUBLANES, (off + k) // SUBLANES * SUBLANES
            taps = sh_ref[s, base + a:base + a + chunk, :].reshape(chunk // SUBLANES, SUBLANES, D_CONV)
            acc = acc + taps * cw_ref[k][None]
        y = acc.reshape(chunk, D_CONV) + cb_ref[...]
        mu = jnp.mean(y, axis=-1, keepdims=True)
        yc = y - mu
        var = jnp.mean(yc * yc, axis=-1, keepdims=True)
        yn = (yc * lax.rsqrt(var + EPS)) * lg_ref[...] + lb_ref[...]
        z = _silu(yn) * sga_ref[base:base + chunk, :]
        zc_ref[0, base:base + chunk, :] = z.astype(zc_ref.dtype)

    q_ref[0] = (head_rms(p_ref[:, 0:D_ATT], qn_ref[...]) * ATT_SCALE).astype(q_ref.dtype)
    k_ref[0] = head_rms(p_ref[:, D_ATT:2 * D_ATT], kn_ref[...]).astype(k_ref.dtype)
    v_ref[0] = p_ref[:, 2 * D_ATT:3 * D_ATT].astype(v_ref.dtype)
    sgb_ref[0] = _silu(p_ref[:, 3 * D_ATT:4 * D_ATT])


def _layer0_head(x3, mod, tile, g0, w_in, qn, kn, bd, conv_w, conv_b, ln_g, ln_b, kv_dtype):
    bsz, L, _ = x3.shape
    n_tiles = L // tile
    n_mod = mod.shape[0]
    const2 = lambda b, i: (0, 0)
    main = lambda width: pl.BlockSpec((1, tile, width), lambda b, i: (b, i, 0))
    if n_tiles == 1:
        x_specs, x_ops = [main(D_MODEL)], [x3]
    else:
        prev, nxt = _halo_specs(tile, CONV_HALO, L, D_MODEL)
        x_specs, x_ops = [prev, main(D_MODEL), nxt], [x3, x3, x3]
    vec = pl.BlockSpec((1, D_CONV), const2)
    out = lambda dtype: jax.ShapeDtypeStruct((bsz, L, D_CONV), dtype)
    ext = tile + 2 * CONV_HALO
    return pl.pallas_call(
        functools.partial(_head_kernel, tile=tile, n_tiles=n_tiles, chunk=32),
        out_shape=(out(BF16),
                   out(BF16),
                   out(kv_dtype),
                   out(kv_dtype),
                   out(F32)),
        grid=(bsz, n_tiles),
        in_specs=x_specs + [
            pl.BlockSpec((1, 3, D_MODEL), lambda b, i: (jnp.minimum(b, n_mod - 1), 0, 0)),
            pl.BlockSpec((1, D_MODEL), const2),
            _resident((D_MODEL, D_IN_EVEN), const2),
            vec, vec,
            pl.BlockSpec((D_ATT, D_ATT), const2),
            pl.BlockSpec((CONV_K, SUBLANES, D_CONV), lambda b, i: (0, 0, 0)),
            vec, vec, vec],
        out_specs=(main(D_CONV),) * 5,
        scratch_shapes=[pltpu.VMEM((ext, D_CONV), F32),
                        pltpu.VMEM((SUBLANES, ext - SUBLANES, D_CONV), F32),
                        pltpu.VMEM((tile, D_CONV), F32),
                        pltpu.VMEM((tile, D_MODEL), BF16),
                        pltpu.VMEM((tile, D_IN_EVEN - 3 * D_CONV), F32)],
        compiler_params=_cparams(2),
        name="layer0_head",
    )(*x_ops, mod, g0, w_in, qn, kn, bd,
      jnp.broadcast_to(conv_w[:, None, :], (CONV_K, SUBLANES, D_CONV)),
      conv_b.reshape(1, D_CONV), ln_g.reshape(1, D_CONV), ln_b.reshape(1, D_CONV))


def _halo_specs(tile, halo, seq, width):
    r = tile // halo
    last = seq // halo - 1
    prev = pl.BlockSpec((1, halo, width), lambda b, i: (b, jnp.maximum(i * r - 1, 0), 0))
    nxt = pl.BlockSpec((1, halo, width), lambda b, i: (b, jnp.minimum((i + 1) * r, last), 0))
    return prev, nxt


def _dot_nt(a, b):
    return lax.dot_general(a, b, (((1,), (1,)), ((), ())), preferred_element_type=F32)


def _head_masks():
    lane = lax.broadcasted_iota(jnp.int32, (1, PAIR_W), 1)
    first = lane < HEAD_DIM
    return (first, jnp.logical_not(first))


def _keep(mask, t):
    return jnp.where(mask, t, jnp.zeros_like(t))


def _ctx_attn_kernel(q_ref, k_ref, v_ref, sgb_ref, z_ref):
    masks = _head_masks()

    def cols(p):
        return slice(p * PAIR_W, (p + 1) * PAIR_W)

    def scores(p, e):
        return _dot_nt(_keep(masks[e], q_ref[0, :, cols(p)]), k_ref[0, :, cols(p)].astype(BF16))

    def finish(p, e, s):
        pr = jnp.exp(s - jnp.max(s, axis=-1, keepdims=True))
        l = jnp.sum(pr, axis=-1, keepdims=True)
        vb = _keep(masks[e], v_ref[0, :, cols(p)].astype(BF16))
        return jnp.dot(pr.astype(BF16), vb, preferred_element_type=F32) / l

    chains = [(p, e) for p in range(N_HEAD_PAIRS) for e in range(HEADS_PER_VREG)]
    done = {}
    s_next = scores(*chains[0])
    for c, (p, e) in enumerate(chains):
        s_cur = s_next
        if c + 1 < len(chains):
            s_next = scores(*chains[c + 1])
        done[(p, e)] = finish(p, e, s_cur)
    for p in range(N_HEAD_PAIRS):
        o = done[(p, 0)] + done[(p, 1)]
        z_ref[0, :, cols(p)] = (o * sgb_ref[0, :, cols(p)]).astype(z_ref.dtype)


def _context_attention(q3, k3, v3, sgb3):
    bsz, L, _ = q3.shape
    blk = pl.BlockSpec((1, L, D_ATT), lambda b: (b, 0, 0))
    return pl.pallas_call(
        _ctx_attn_kernel,
        out_shape=jax.ShapeDtypeStruct((bsz, L, D_ATT), BF16),
        grid=(bsz,),
        in_specs=[blk, blk, blk, blk],
        out_specs=blk,
        compiler_params=_cparams(1),
        name="context_attention",
    )(q3, k3, v3, sgb3)


def _na_kernel(q_ref, k_ref, v_ref, kc_ref, vc_ref, bias_ref, sgb_ref, z_ref, kcb_ref, vcb_ref,
               *, rows_per_step, n_rows):
    kcb_ref[...] = kc_ref[0].astype(BF16)
    vcb_ref[...] = vc_ref[0].astype(BF16)

    def group(rt, carry):
        _na_row_group(rt, q_ref, k_ref, v_ref, kcb_ref, vcb_ref, bias_ref, sgb_ref, z_ref,
                      rows_per_step=rows_per_step, n_rows=n_rows)
        return carry

    lax.fori_loop(0, n_rows // rows_per_step, group, 0)


def _na_row_group(rt, q_ref, k_ref, v_ref, kcb_ref, vcb_ref, bias_ref, sgb_ref, z_ref,
                  *, rows_per_step, n_rows):
    masks = _head_masks()
    kc = kcb_ref[...]
    vc = vcb_ref[...]
    win = NA_ROWS * GRID_W
    stack = HEADS_PER_VREG * GRID_W

    def q_rows(j):
        return pl.ds(pl.multiple_of((rt * rows_per_step + j) * GRID_W, GRID_W), GRID_W)

    q_stack = jnp.concatenate(
        [_keep(hm, q_ref[0, q_rows(j), :]) for j in range(rows_per_step) for hm in masks], axis=0)
    s_ctx = _dot_nt(q_stack, kc)

    def window(j):
        r = rt * rows_per_step + j
        r_start = jnp.clip(r - NA_ROWS // 2, 0, n_rows - NA_ROWS)
        ws = pl.ds(pl.multiple_of(r_start * GRID_W, GRID_W), win)
        return r - r_start, ws

    def scores(j):
        r_off, ws = window(j)
        return _dot_nt(q_stack[j * stack:(j + 1) * stack], k_ref[0, ws, :]) + bias_ref[r_off, 0]

    def finish(j, s_win):
        _, ws = window(j)
        sc = s_ctx[j * stack:(j + 1) * stack]
        m = jnp.maximum(jnp.max(s_win, axis=-1, keepdims=True), jnp.max(sc, axis=-1, keepdims=True))
        p_win = jnp.exp(s_win - m)
        p_ctx = jnp.exp(sc - m)
        l = jnp.sum(p_win, axis=-1, keepdims=True) + jnp.sum(p_ctx, axis=-1, keepdims=True)
        o_win = jnp.dot(p_win.astype(BF16), v_ref[0, ws, :], preferred_element_type=F32)
        return o_win, p_ctx.astype(BF16), l

    done = []
    ahead = 2
    pending = [scores(j) for j in range(min(ahead, rows_per_step))]
    for j in range(rows_per_step):
        if j + ahead < rows_per_step:
            pending.append(scores(j + ahead))
        done.append(finish(j, pending.pop(0)))

    o_win = jnp.concatenate([t[0] for t in done], axis=0)
    p_ctx = jnp.concatenate([t[1] for t in done], axis=0)
    l = jnp.concatenate([t[2] for t in done], axis=0)
    o = (o_win + jnp.dot(p_ctx, vc, preferred_element_type=F32)) / l
    for j in range(rows_per_step):
        oj = jnp.where(masks[0], o[j * stack:j * stack + GRID_W], o[j * stack + GRID_W:(j + 1) * stack])
        z_ref[0, q_rows(j), :] = (oj * sgb_ref[0, q_rows(j), :]).astype(z_ref.dtype)


def _window_bias(rpb):
    qc = jnp.arange(GRID_W)
    kcol = jnp.arange(GRID_W)
    start = jnp.clip(qc - NA_COLS // 2, 0, GRID_W - NA_COLS)
    ok = (kcol[None, :] >= start[:, None]) & (kcol[None, :] < start[:, None] + NA_COLS)
    rel_c = kcol[None, :] - qc[:, None] + NA_COLS - 1
    r_off = jnp.arange(NA_ROWS)
    rel_r = jnp.arange(NA_ROWS)[None, :] - r_off[:, None] + NA_ROWS - 1
    onehot = ((rel_c[None] == jnp.arange(2 * NA_COLS - 1)[:, None, None]) & ok[None]).astype(F32)
    rows = rpb.astype(F32)[:, rel_r]
    b = jnp.einsum('hrjd,dqk->rhqjk', rows, onehot, precision=lax.Precision.HIGHEST)
    b = jnp.where(ok[None, None, :, None, :], b, MASK_BIAS)
    return b.reshape(NA_ROWS, N_HEAD_PAIRS, HEADS_PER_VREG * GRID_W, NA_ROWS * GRID_W)


def _neighbourhood_attention(q3, k3, v3, kc3, vc3, bias, sgb3, rows_per_step):
    bsz, L, _ = q3.shape
    n_rows = L // GRID_W
    past = kc3.shape[1]
    full = pl.BlockSpec((1, L, PAIR_W), lambda b, p: (b, 0, p))
    ctx = pl.BlockSpec((1, past, PAIR_W), lambda b, p: (b, 0, p))
    bias_spec = pl.BlockSpec((NA_ROWS, 1, HEADS_PER_VREG * GRID_W, NA_ROWS * GRID_W),
                             lambda b, p: (0, p, 0, 0))
    return pl.pallas_call(
        functools.partial(_na_kernel, rows_per_step=rows_per_step, n_rows=n_rows),
        out_shape=jax.ShapeDtypeStruct((bsz, L, D_ATT), BF16),
        grid=(bsz, N_HEAD_PAIRS),
        in_specs=[full, full, full, ctx, ctx, bias_spec, full],
        out_specs=full,
        scratch_shapes=[pltpu.VMEM((past, PAIR_W), BF16), pltpu.VMEM((past, PAIR_W), BF16)],
        compiler_params=_cparams(2),
        name="neighbourhood_attention",
    )(q3, k3, v3, kc3, vc3, bias, sgb3)


def _shift_up(x, k):
    n = x.shape[0]
    return pltpu.roll(x, n - k, 0)


def _pool_window_sums(u, w):
    c = u
    m = 1
    while 2 * m < w:
        c = c + _shift_up(c, m)
        m *= 2
    return pltpu.roll(c, w // 2, 0) + c


def _tail_kernel(*refs, tile, n_tiles, seq):
    if n_tiles > 1:
        (xp_ref, x_ref, xn_ref, cp_ref, zc_ref, cn_ref, ap_ref, za_ref, an_ref,
         mod0_ref, mod1_ref, g1_ref, wo0_ref, w1_ref, pw_ref, ps_ref, wo1_ref, o_ref) = refs
        ext = lambda p, m, n: jnp.concatenate([p[0], m[0], n[0]], axis=0)
        x, zc, za = ext(xp_ref, x_ref, xn_ref), ext(cp_ref, zc_ref, cn_ref), ext(ap_ref, za_ref, an_ref)
    else:
        (x_ref, zc_ref, za_ref,
         mod0_ref, mod1_ref, g1_ref, wo0_ref, w1_ref, pw_ref, ps_ref, wo1_ref, o_ref) = refs
        x, zc, za = x_ref[0], zc_ref[0], za_ref[0]
    i = pl.program_id(1)
    h = TAIL_HALO
    out0 = (jnp.dot(zc, wo0_ref[0:D_CONV, :], preferred_element_type=F32)
            + jnp.dot(za, wo0_ref[D_CONV:D_CONV + D_ATT, :], preferred_element_type=F32))
    y = x + mod0_ref[0, 2:3, :] * out0
    hb = _modulated_norm(y, g1_ref[...], mod1_ref).astype(BF16)
    u = jnp.dot(hb, w1_ref[:, 0:D_MODEL], preferred_element_type=F32)
    if n_tiles > 1:
        row = lax.broadcasted_iota(jnp.int32, (tile + 2 * h, 1), 0)
        outside = ((row < h) & (i == 0)) | ((row >= h + tile) & (i == n_tiles - 1))
        u = jnp.where(outside, 0.0, u)
        y = y[h:h + tile]
        hb = hb[h:h + tile]
    else:
        pad = jnp.zeros((h, D_MODEL), F32)
        u = jnp.concatenate([pad, u, pad], axis=0)
    sg = _silu(jnp.dot(hb, w1_ref[:, D_MODEL:2 * D_MODEL], preferred_element_type=F32))
    t = (i * tile).astype(F32) + lax.broadcasted_iota(jnp.int32, (tile, 1), 0).astype(F32)
    out1 = jnp.zeros((tile, D_MODEL), F32)
    for g, w in enumerate(POOL_WINDOWS):
        cols = slice(g * POOL_GD, (g + 1) * POOL_GD)
        ug = u[:, cols]
        s = _pool_window_sums(ug, w)[h:h + tile]
        cnt = jnp.minimum(t + (w - w // 2), float(seq)) - jnp.maximum(t - w // 2, 0.0)
        d = s / cnt - ug[h:h + tile]
        yp = jnp.dot(d.astype(BF16), pw_ref[g], preferred_element_type=F32) * ps_ref[:, cols]
        z = (yp * sg[:, cols]).astype(BF16)
        out1 = out1 + jnp.dot(z, wo1_ref[cols, :], preferred_element_type=F32)
    o_ref[0] = y + mod1_ref[0, 2:3, :] * out1


def _layers_tail(x3, zc3, za3, mod0, mod1, tile, g1, w_out0, w_in1, pool_w, pool_scale, w_out1):
    bsz, L, _ = x3.shape
    n_tiles = L // tile
    n_mod = mod0.shape[0]
    const2 = lambda b, i: (0, 0)
    modspec = pl.BlockSpec((1, 3, D_MODEL), lambda b, i: (jnp.minimum(b, n_mod - 1), 0, 0))

    def tiled(width):
        main = pl.BlockSpec((1, tile, width), lambda b, i: (b, i, 0))
        if n_tiles == 1:
            return [main]
        prev, nxt = _halo_specs(tile, TAIL_HALO, L, width)
        return [prev, main, nxt]

    def operands(t):
        return [t] if n_tiles == 1 else [t, t, t]

    return pl.pallas_call(
        functools.partial(_tail_kernel, tile=tile, n_tiles=n_tiles, seq=L),
        out_shape=jax.ShapeDtypeStruct((bsz, L, D_MODEL), F32),
        grid=(bsz, n_tiles),
        in_specs=tiled(D_MODEL) + tiled(D_CONV) + tiled(D_ATT) + [
            modspec, modspec,
            pl.BlockSpec((1, D_MODEL), const2),
            _resident((D_CONV + D_ATT, D_MODEL), const2),
            _resident((D_MODEL, 2 * D_MODEL), const2),
            _resident((len(POOL_WINDOWS), POOL_GD, POOL_GD), lambda b, i: (0, 0, 0)),
            pl.BlockSpec((1, D_MODEL), const2),
            _resident((D_MODEL, D_MODEL), const2)],
        out_specs=pl.BlockSpec((1, tile, D_MODEL), lambda b, i: (b, i, 0)),
        compiler_params=_cparams(2),
        name="layers_tail",
    )(*operands(x3), *operands(zc3), *operands(za3), mod0, mod1, g1, w_out0, w_in1, pool_w,
      pool_scale.reshape(1, D_MODEL), w_out1)


def _path(x, mod0, mod1, head_tile, tail_tile, params, attention):
    (g0, w_in0, conv_w, conv_b, ln_g, ln_b, qn, kn, bd, w_out0, g1, w_in1, pool_w, pool_scale,
     w_out1, kv_dtype) = params
    zc, q, k, v, sgb = _layer0_head(x, mod0, head_tile, g0, w_in0, qn, kn, bd, conv_w, conv_b, ln_g, ln_b,
                                    kv_dtype)
    za = attention(q, k, v, sgb)
    out = _layers_tail(x, zc, za, mod0, mod1, tail_tile, g1, w_out0, w_in1, pool_w, pool_scale, w_out1)
    return out, k, v


def kernel(x_prompt, x_sample, cache_k_0, cache_v_0, c, c_ctx, norm_g_0, w_ada_0, b_ada_0, w_in_0, conv_w_0, conv_b_0, conv_ln_g_0, conv_ln_b_0, q_norm_0, k_norm_0, rpb_0, w_out_0, norm_g_1, w_ada_1, b_ada_1, w_in_1, pool_w_1, pool_scale_1, w_out_1):
    bp, lp, _ = x_prompt.shape
    bs, ls, _ = x_sample.shape
    past = cache_k_0.shape[1]

    cond = jnp.concatenate([c_ctx[None, :], c, jnp.zeros((ADA_ROWS - 1 - bs, D_MODEL), F32)], axis=0)
    m0, m1 = _ada_modulation(cond, w_ada_0, b_ada_0, w_ada_1, b_ada_1)
    split = lambda m, lo, hi: m[lo:hi].reshape(hi - lo, 3, D_MODEL)
    mod0_p, mod1_p = split(m0, 0, 1), split(m1, 0, 1)
    mod0_s, mod1_s = split(m0, 1, 1 + bs), split(m1, 1, 1 + bs)

    head_id = jnp.arange(D_ATT) // HEAD_DIM
    bd = (head_id[:, None] == head_id[None, :]).astype(BF16)
    common = (norm_g_0.reshape(1, D_MODEL), w_in_0.astype(BF16), conv_w_0, conv_b_0, conv_ln_g_0,
              conv_ln_b_0, jnp.tile(q_norm_0, NA_HEADS).reshape(1, D_ATT),
              jnp.tile(k_norm_0, NA_HEADS).reshape(1, D_ATT), bd, w_out_0.astype(BF16),
              norm_g_1.reshape(1, D_MODEL), w_in_1.astype(BF16), pool_w_1.astype(BF16), pool_scale_1,
              w_out_1.astype(BF16))

    y_prompt, k_ctx, v_ctx = _path(x_prompt, mod0_p, mod1_p, lp, lp, common + (F32,), _context_attention)

    bias = _window_bias(rpb_0)
    kc3 = cache_k_0.reshape(bs, past, D_ATT)
    vc3 = cache_v_0.reshape(bs, past, D_ATT)
    na = lambda q3, k3, v3, sgb3: _neighbourhood_attention(q3, k3, v3, kc3, vc3, bias, sgb3, 8)
    y_sample, _, _ = _path(x_sample, mod0_s, mod1_s, 512, 512, common + (BF16,), na)

    return (y_prompt, y_sample,
            k_ctx.reshape(bp, lp, NA_HEADS, HEAD_DIM), v_ctx.reshape(bp, lp, NA_HEADS, HEAD_DIM))
```

```python
import functools

import jax
import jax.numpy as jnp
from jax import lax
from jax.experimental import pallas as pl
from jax.experimental.pallas import tpu as pltpu

F32 = jnp.float32
BF16 = jnp.bfloat16

D_MODEL = 1024
EPS = 1e-6
D_CONV = 512
CONV_K = 31
CONV_HALO = 16
NA_HEADS = 8
HEAD_DIM = 64
D_ATT = NA_HEADS * HEAD_DIM
HEADS_PER_VREG = 2
N_HEAD_PAIRS = NA_HEADS // HEADS_PER_VREG
PAIR_W = HEADS_PER_VREG * HEAD_DIM
GRID_W = 64
NA_ROWS = 8
NA_COLS = 16
ATT_SCALE = HEAD_DIM ** -0.5
D_IN_EVEN = 3 * D_CONV + 4 * D_ATT
POOL_WINDOWS = (2, 4, 8, 16)
POOL_GD = D_MODEL // len(POOL_WINDOWS)
TAIL_HALO = 16
MASK_BIAS = -1e30
SUBLANES = 8
MXU_WIDTH = 256
ADA_ROWS = SUBLANES
VMEM_LIMIT = 48 * 1024 * 1024


def _sigmoid(x):
    return 1.0 / (1.0 + jnp.exp(-x))


def _silu(x):
    return x * _sigmoid(x)


def _resident(shape, index_map):
    return pl.BlockSpec(shape, index_map, pipeline_mode=pl.Buffered(1))


def _cparams(n_axes):
    return pltpu.CompilerParams(dimension_semantics=("parallel",) * n_axes,
                                vmem_limit_bytes=VMEM_LIMIT)


def _ada_kernel(cond_ref, w0_ref, b0_ref, w1_ref, b1_ref, m0_ref, m1_ref):
    s = _silu(cond_ref[...])
    m0_ref[...] = jnp.dot(s, w0_ref[...], preferred_element_type=F32,
                          precision=lax.Precision.HIGHEST) + b0_ref[...]
    m1_ref[...] = jnp.dot(s, w1_ref[...], preferred_element_type=F32,
                          precision=lax.Precision.HIGHEST) + b1_ref[...]


def _ada_modulation(cond, w_ada_0, b_ada_0, w_ada_1, b_ada_1):
    tn = 512
    n3 = 3 * D_MODEL
    wspec = pl.BlockSpec((D_MODEL, tn), lambda j: (0, j))
    bspec = pl.BlockSpec((1, tn), lambda j: (0, j))
    ospec = pl.BlockSpec((ADA_ROWS, tn), lambda j: (0, j))
    return pl.pallas_call(
        _ada_kernel,
        out_shape=(jax.ShapeDtypeStruct((ADA_ROWS, n3), F32),) * 2,
        grid=(n3 // tn,),
        in_specs=[pl.BlockSpec((ADA_ROWS, D_MODEL), lambda j: (0, 0)), wspec, bspec, wspec, bspec],
        out_specs=(ospec, ospec),
        compiler_params=_cparams(1),
        name="ada_modulation",
    )(cond, w_ada_0, b_ada_0.reshape(1, n3), w_ada_1, b_ada_1.reshape(1, n3))


def _modulated_norm(x, g_row, mod_ref):
    ms = jnp.mean(x * x, axis=-1, keepdims=True)
    y = (x * lax.rsqrt(ms + EPS)) * g_row
    return y * (1.0 + mod_ref[0, 1:2, :]) + mod_ref[0, 0:1, :]


def _head_kernel(*refs, tile, n_tiles, chunk):
    if n_tiles > 1:
        xp_ref, x_ref, xn_ref = refs[:3]
        refs = refs[3:]
        x = jnp.concatenate([xp_ref[0], x_ref[0], xn_ref[0]], axis=0)
    else:
        x_ref = refs[0]
        refs = refs[1:]
        x = x_ref[0]
    (mod_ref, g_ref, w_ref, qn_ref, kn_ref, bd_ref, cw_ref, cb_ref, lg_ref, lb_ref,
     zc_ref, q_ref, k_ref, v_ref, sgb_ref, buf_ref, sh_ref, sga_ref, hb_ref, p_ref) = refs
    i = pl.program_id(1)
    h = CONV_HALO
    hb = _modulated_norm(x, g_ref[...], mod_ref).astype(BF16)

    def proj(t, c):
        return jnp.dot(t, w_ref[:, c * D_CONV:(c + 1) * D_CONV], preferred_element_type=F32)

    def head_rms(t, g_row):
        ms = jnp.dot((t * t).astype(BF16), bd_ref[...], preferred_element_type=F32) * (1.0 / HEAD_DIM)
        return (t * lax.rsqrt(ms + EPS)) * g_row

    u = proj(hb, 0) * _sigmoid(proj(hb, 1))
    if n_tiles > 1:
        row = lax.broadcasted_iota(jnp.int32, (tile + 2 * h, 1), 0)
        outside = ((row < h) & (i == 0)) | ((row >= h + tile) & (i == n_tiles - 1))
        buf_ref[...] = jnp.where(outside, 0.0, u)
        hb = hb[h:h + tile]
    else:
        pad = jnp.zeros((h, D_CONV), F32)
        buf_ref[0:h, :] = pad
        buf_ref[h:h + tile, :] = u
        buf_ref[h + tile:2 * h + tile, :] = pad
    span = tile + 2 * h - SUBLANES
    for s in range(SUBLANES):
        sh_ref[s, 0:span, :] = buf_ref[s:s + span, :]
    sga_ref[...] = _silu(proj(hb, 2))
    hb_ref[...] = hb
    off = h - CONV_K // 2
    n_chunks = tile // chunk
    rest = D_IN_EVEN - 3 * D_CONV
    piece = MXU_WIDTH
    chunks_per_piece = n_chunks * piece // rest

    never = lax.broadcasted_iota(jnp.int32, (chunk, D_CONV), 0) < jnp.minimum(pl.program_id(0), 0)
    for c in range(n_chunks):
        base = c * chunk
        if c % chunks_per_piece == 0:
            col = c // chunks_per_piece * piece
            pp = jnp.dot(hb_ref[...], w_ref[:, 3 * D_CONV + col:3 * D_CONV + col + piece],
                         preferred_element_type=F32)
            p_ref[:, col:col + piece] = pp
        anchor = jnp.concatenate([pp[r * chunk:(r + 1) * chunk] for r in range(D_CONV // piece)], axis=1)
        acc = jnp.where(never, anchor, 0.0).reshape(chunk // SUBLANES, SUBLANES, D_CONV)
        for k in range(CONV_K):
            s, a = (off + k) % SUBLANES, (off + k) // SUBLANES * SUBLANES
            taps = sh_ref[s, base + a:base + a + chunk, :].reshape(chunk // SUBLANES, SUBLANES, D_CONV)
            acc = acc + taps * cw_ref[k][None]
        y = acc.reshape(chunk, D_CONV) + cb_ref[...]
        mu = jnp.mean(y, axis=-1, keepdims=True)
        yc = y - mu
        var = jnp.mean(yc * yc, axis=-1, keepdims=True)
        yn = (yc * lax.rsqrt(var + EPS)) * lg_ref[...] + lb_ref[...]
        z = _silu(yn) * sga_ref[base:base + chunk, :]
        zc_ref[0, base:base + chunk, :] = z.astype(zc_ref.dtype)

    q_ref[0] = (head_rms(p_ref[:, 0:D_ATT], qn_ref[...]) * ATT_SCALE).astype(q_ref.dtype)
    k_ref[0] = head_rms(p_ref[:, D_ATT:2 * D_ATT], kn_ref[...]).astype(k_ref.dtype)
    v_ref[0] = p_ref[:, 2 * D_ATT:3 * D_ATT].astype(v_ref.dtype)
    sgb_ref[0] = _silu(p_ref[:, 3 * D_ATT:4 * D_ATT])


def _layer0_head(x3, mod, tile, g0, w_in, qn, kn, bd, conv_w, conv_b, ln_g, ln_b, kv_dtype):
    bsz, L, _ = x3.shape
    n_tiles = L // tile
    n_mod = mod.shape[0]
    const2 = lambda b, i: (0, 0)
    main = lambda width: pl.BlockSpec((1, tile, width), lambda b, i: (b, i, 0))
    if n_tiles == 1:
        x_specs, x_ops = [main(D_MODEL)], [x3]
    else:
        prev, nxt = _halo_specs(tile, CONV_HALO, L, D_MODEL)
        x_specs, x_ops = [prev, main(D_MODEL), nxt], [x3, x3, x3]
    vec = pl.BlockSpec((1, D_CONV), const2)
    out = lambda dtype: jax.ShapeDtypeStruct((bsz, L, D_CONV), dtype)
    ext = tile + 2 * CONV_HALO
    return pl.pallas_call(
        functools.partial(_head_kernel, tile=tile, n_tiles=n_tiles, chunk=32),
        out_shape=(out(BF16),
                   out(BF16),
                   out(kv_dtype),
                   out(kv_dtype),
                   out(F32)),
        grid=(bsz, n_tiles),
        in_specs=x_specs + [
            pl.BlockSpec((1, 3, D_MODEL), lambda b, i: (jnp.minimum(b, n_mod - 1), 0, 0)),
            pl.BlockSpec((1, D_MODEL), const2),
            _resident((D_MODEL, D_IN_EVEN), const2),
            vec, vec,
            pl.BlockSpec((D_ATT, D_ATT), const2),
            pl.BlockSpec((CONV_K, SUBLANES, D_CONV), lambda b, i: (0, 0, 0)),
            vec, vec, vec],
        out_specs=(main(D_CONV),) * 5,
        scratch_shapes=[pltpu.VMEM((ext, D_CONV), F32),
                        pltpu.VMEM((SUBLANES, ext - SUBLANES, D_CONV), F32),
                        pltpu.VMEM((tile, D_CONV), F32),
                        pltpu.VMEM((tile, D_MODEL), BF16),
                        pltpu.VMEM((tile, D_IN_EVEN - 3 * D_CONV), F32)],
        compiler_params=_cparams(2),
        name="layer0_head",
    )(*x_ops, mod, g0, w_in, qn, kn, bd,
      jnp.broadcast_to(conv_w[:, None, :], (CONV_K, SUBLANES, D_CONV)),
      conv_b.reshape(1, D_CONV), ln_g.reshape(1, D_CONV), ln_b.reshape(1, D_CONV))


def _halo_specs(tile, halo, seq, width):
    r = tile // halo
    last = seq // halo - 1
    prev = pl.BlockSpec((1, halo, width), lambda b, i: (b, jnp.maximum(i * r - 1, 0), 0))
    nxt = pl.BlockSpec((1, halo, width), lambda b, i: (b, jnp.minimum((i + 1) * r, last), 0))
    return prev, nxt


def _dot_nt(a, b):
    return lax.dot_general(a, b, (((1,), (1,)), ((), ())), preferred_element_type=F32)


def _head_masks():
    lane = lax.broadcasted_iota(jnp.int32, (1, PAIR_W), 1)
    first = lane < HEAD_DIM
    return (first, jnp.logical_not(first))


def _keep(mask, t):
    return jnp.where(mask, t, jnp.zeros_like(t))


def _ctx_attn_kernel(q_ref, k_ref, v_ref, sgb_ref, z_ref):
    masks = _head_masks()

    def cols(p):
        return slice(p * PAIR_W, (p + 1) * PAIR_W)

    def scores(p, e):
        return _dot_nt(_keep(masks[e], q_ref[0, :, cols(p)]), k_ref[0, :, cols(p)].astype(BF16))

    def finish(p, e, s):
        pr = jnp.exp(s - jnp.max(s, axis=-1, keepdims=True))
        l = jnp.sum(pr, axis=-1, keepdims=True)
        vb = _keep(masks[e], v_ref[0, :, cols(p)].astype(BF16))
        return jnp.dot(pr.astype(BF16), vb, preferred_element_type=F32) / l

    chains = [(p, e) for p in range(N_HEAD_PAIRS) for e in range(HEADS_PER_VREG)]
    done = {}
    s_next = scores(*chains[0])
    for c, (p, e) in enumerate(chains):
        s_cur = s_next
        if c + 1 < len(chains):
            s_next = scores(*chains[c + 1])
        done[(p, e)] = finish(p, e, s_cur)
    for p in range(N_HEAD_PAIRS):
        o = done[(p, 0)] + done[(p, 1)]
        z_ref[0, :, cols(p)] = (o * sgb_ref[0, :, cols(p)]).astype(z_ref.dtype)


def _context_attention(q3, k3, v3, sgb3):
    bsz, L, _ = q3.shape
    blk = pl.BlockSpec((1, L, D_ATT), lambda b: (b, 0, 0))
    return pl.pallas_call(
        _ctx_attn_kernel,
        out_shape=jax.ShapeDtypeStruct((bsz, L, D_ATT), BF16),
        grid=(bsz,),
        in_specs=[blk, blk, blk, blk],
        out_specs=blk,
        compiler_params=_cparams(1),
        name="context_attention",
    )(q3, k3, v3, sgb3)


def _na_kernel(q_ref, k_ref, v_ref, kc_ref, vc_ref, bias_ref, sgb_ref, z_ref, kcb_ref, vcb_ref,
               *, rows_per_step, n_rows):
    kcb_ref[...] = kc_ref[0].astype(BF16)
    vcb_ref[...] = vc_ref[0].astype(BF16)

    def group(rt, carry):
        _na_row_group(rt, q_ref, k_ref, v_ref, kcb_ref, vcb_ref, bias_ref, sgb_ref, z_ref,
                      rows_per_step=rows_per_step, n_rows=n_rows)
        return carry

    lax.fori_loop(0, n_rows // rows_per_step, group, 0)


def _na_row_group(rt, q_ref, k_ref, v_ref, kcb_ref, vcb_ref, bias_ref, sgb_ref, z_ref,
                  *, rows_per_step, n_rows):
    masks = _head_masks()
    kc = kcb_ref[...]
    vc = vcb_ref[...]
    win = NA_ROWS * GRID_W
    stack = HEADS_PER_VREG * GRID_W

    def q_rows(j):
        return pl.ds(pl.multiple_of((rt * rows_per_step + j) * GRID_W, GRID_W), GRID_W)

    q_stack = jnp.concatenate(
        [_keep(hm, q_ref[0, q_rows(j), :]) for j in range(rows_per_step) for hm in masks], axis=0)
    s_ctx = _dot_nt(q_stack, kc)

    def window(j):
        r = rt * rows_per_step + j
        r_start = jnp.clip(r - NA_ROWS // 2, 0, n_rows - NA_ROWS)
        ws = pl.ds(pl.multiple_of(r_start * GRID_W, GRID_W), win)
        return r - r_start, ws

    def scores(j):
        r_off, ws = window(j)
        return _dot_nt(q_stack[j * stack:(j + 1) * stack], k_ref[0, ws, :]) + bias_ref[r_off, 0]

    def finish(j, s_win):
        _, ws = window(j)
        sc = s_ctx[j * stack:(j + 1) * stack]
        m = jnp.maximum(jnp.max(s_win, axis=-1, keepdims=True), jnp.max(sc, axis=-1, keepdims=True))
        p_win = jnp.exp(s_win - m)
        p_ctx = jnp.exp(sc - m)
        l = jnp.sum(p_win, axis=-1, keepdims=True) + jnp.sum(p_ctx, axis=-1, keepdims=True)
        o_win = jnp.dot(p_win.astype(BF16), v_ref[0, ws, :], preferred_element_type=F32)
        return o_win, p_ctx.astype(BF16), l

    done = []
    ahead = 2
    pending = [scores(j) for j in range(min(ahead, rows_per_step))]
    for j in range(rows_per_step):
        if j + ahead < rows_per_step:
            pending.append(scores(j + ahead))
        done.append(finish(j, pending.pop(0)))

    o_win = jnp.concatenate([t[0] for t in done], axis=0)
    p_ctx = jnp.concatenate([t[1] for t in done], axis=0)
    l = jnp.concatenate([t[2] for t in done], axis=0)
    o = (o_win + jnp.dot(p_ctx, vc, preferred_element_type=F32)) / l
    for j in range(rows_per_step):
        oj = jnp.where(masks[0], o[j * stack:j * stack + GRID_W], o[j * stack + GRID_W:(j + 1) * stack])
        z_ref[0, q_rows(j), :] = (oj * sgb_ref[0, q_rows(j), :]).astype(z_ref.dtype)


def _window_bias(rpb):
    qc = jnp.arange(GRID_W)
    kcol = jnp.arange(GRID_W)
    start = jnp.clip(qc - NA_COLS // 2, 0, GRID_W - NA_COLS)
    ok = (kcol[None, :] >= start[:, None]) & (kcol[None, :] < start[:, None] + NA_COLS)
    rel_c = kcol[None, :] - qc[:, None] + NA_COLS - 1
    r_off = jnp.arange(NA_ROWS)
    rel_r = jnp.arange(NA_ROWS)[None, :] - r_off[:, None] + NA_ROWS - 1
    onehot = ((rel_c[None] == jnp.arange(2 * NA_COLS - 1)[:, None, None]) & ok[None]).astype(F32)
    rows = rpb.astype(F32)[:, rel_r]
    b = jnp.einsum('hrjd,dqk->rhqjk', rows, onehot, precision=lax.Precision.HIGHEST)
    b = jnp.where(ok[None, None, :, None, :], b, MASK_BIAS)
    return b.reshape(NA_ROWS, N_HEAD_PAIRS, HEADS_PER_VREG * GRID_W, NA_ROWS * GRID_W)


def _neighbourhood_attention(q3, k3, v3, kc3, vc3, bias, sgb3, rows_per_step):
    bsz, L, _ = q3.shape
    n_rows = L // GRID_W
    past = kc3.shape[1]
    full = pl.BlockSpec((1, L, PAIR_W), lambda b, p: (b, 0, p))
    ctx = pl.BlockSpec((1, past, PAIR_W), lambda b, p: (b, 0, p))
    bias_spec = pl.BlockSpec((NA_ROWS, 1, HEADS_PER_VREG * GRID_W, NA_ROWS * GRID_W),
                             lambda b, p: (0, p, 0, 0))
    return pl.pallas_call(
        functools.partial(_na_kernel, rows_per_step=rows_per_step, n_rows=n_rows),
        out_shape=jax.ShapeDtypeStruct((bsz, L, D_ATT), BF16),
        grid=(bsz, N_HEAD_PAIRS),
        in_specs=[full, full, full, ctx, ctx, bias_spec, full],
        out_specs=full,
        scratch_shapes=[pltpu.VMEM((past, PAIR_W), BF16), pltpu.VMEM((past, PAIR_W), BF16)],
        compiler_params=_cparams(2),
        name="neighbourhood_attention",
    )(q3, k3, v3, kc3, vc3, bias, sgb3)


def _shift_up(x, k):
    n = x.shape[0]
    return pltpu.roll(x, n - k, 0)


def _pool_window_sums(u, w):
    c = u
    m = 1
    while 2 * m < w:
        c = c + _shift_up(c, m)
        m *= 2
    return pltpu.roll(c, w // 2, 0) + c


def _tail_kernel(*refs, tile, n_tiles, seq):
    if n_tiles > 1:
        (xp_ref, x_ref, xn_ref, cp_ref, zc_ref, cn_ref, ap_ref, za_ref, an_ref,
         mod0_ref, mod1_ref, g1_ref, wo0_ref, w1_ref, pw_ref, ps_ref, wo1_ref, o_ref) = refs
        ext = lambda p, m, n: jnp.concatenate([p[0], m[0], n[0]], axis=0)
        x, zc, za = ext(xp_ref, x_ref, xn_ref), ext(cp_ref, zc_ref, cn_ref), ext(ap_ref, za_ref, an_ref)
    else:
        (x_ref, zc_ref, za_ref,
         mod0_ref, mod1_ref, g1_ref, wo0_ref, w1_ref, pw_ref, ps_ref, wo1_ref, o_ref) = refs
        x, zc, za = x_ref[0], zc_ref[0], za_ref[0]
    i = pl.program_id(1)
    h = TAIL_HALO
    out0 = (jnp.dot(zc, wo0_ref[0:D_CONV, :], preferred_element_type=F32)
            + jnp.dot(za, wo0_ref[D_CONV:D_CONV + D_ATT, :], preferred_element_type=F32))
    y = x + mod0_ref[0, 2:3, :] * out0
    hb = _modulated_norm(y, g1_ref[...], mod1_ref).astype(BF16)
    u = jnp.dot(hb, w1_ref[:, 0:D_MODEL], preferred_element_type=F32)
    if n_tiles > 1:
        row = lax.broadcasted_iota(jnp.int32, (tile + 2 * h, 1), 0)
        outside = ((row < h) & (i == 0)) | ((row >= h + tile) & (i == n_tiles - 1))
        u = jnp.where(outside, 0.0, u)
        y = y[h:h + tile]
        hb = hb[h:h + tile]
    else:
        pad = jnp.zeros((h, D_MODEL), F32)
        u = jnp.concatenate([pad, u, pad], axis=0)
    sg = _silu(jnp.dot(hb, w1_ref[:, D_MODEL:2 * D_MODEL], preferred_element_type=F32))
    t = (i * tile).astype(F32) + lax.broadcasted_iota(jnp.int32, (tile, 1), 0).astype(F32)
    out1 = jnp.zeros((tile, D_MODEL), F32)
    for g, w in enumerate(POOL_WINDOWS):
        cols = slice(g * POOL_GD, (g + 1) * POOL_GD)
        ug = u[:, cols]
        s = _pool_window_sums(ug, w)[h:h + tile]
        cnt = jnp.minimum(t + (w - w // 2), float(seq)) - jnp.maximum(t - w // 2, 0.0)
        d = s / cnt - ug[h:h + tile]
        yp = jnp.dot(d.astype(BF16), pw_ref[g], preferred_element_type=F32) * ps_ref[:, cols]
        z = (yp * sg[:, cols]).astype(BF16)
        out1 = out1 + jnp.dot(z, wo1_ref[cols, :], preferred_element_type=F32)
    o_ref[0] = y + mod1_ref[0, 2:3, :] * out1


def _layers_tail(x3, zc3, za3, mod0, mod1, tile, g1, w_out0, w_in1, pool_w, pool_scale, w_out1):
    bsz, L, _ = x3.shape
    n_tiles = L // tile
    n_mod = mod0.shape[0]
    const2 = lambda b, i: (0, 0)
    modspec = pl.BlockSpec((1, 3, D_MODEL), lambda b, i: (jnp.minimum(b, n_mod - 1), 0, 0))

    def tiled(width):
        main = pl.BlockSpec((1, tile, width), lambda b, i: (b, i, 0))
        if n_tiles == 1:
            return [main]
        prev, nxt = _halo_specs(tile, TAIL_HALO, L, width)
        return [prev, main, nxt]

    def operands(t):
        return [t] if n_tiles == 1 else [t, t, t]

    return pl.pallas_call(
        functools.partial(_tail_kernel, tile=tile, n_tiles=n_tiles, seq=L),
        out_shape=jax.ShapeDtypeStruct((bsz, L, D_MODEL), F32),
        grid=(bsz, n_tiles),
        in_specs=tiled(D_MODEL) + tiled(D_CONV) + tiled(D_ATT) + [
            modspec, modspec,
            pl.BlockSpec((1, D_MODEL), const2),
            _resident((D_CONV + D_ATT, D_MODEL), const2),
            _resident((D_MODEL, 2 * D_MODEL), const2),
            _resident((len(POOL_WINDOWS), POOL_GD, POOL_GD), lambda b, i: (0, 0, 0)),
            pl.BlockSpec((1, D_MODEL), const2),
            _resident((D_MODEL, D_MODEL), const2)],
        out_specs=pl.BlockSpec((1, tile, D_MODEL), lambda b, i: (b, i, 0)),
        compiler_params=_cparams(2),
        name="layers_tail",
    )(*operands(x3), *operands(zc3), *operands(za3), mod0, mod1, g1, w_out0, w_in1, pool_w,
      pool_scale.reshape(1, D_MODEL), w_out1)


def _path(x, mod0, mod1, head_tile, tail_tile, params, attention):
    (g0, w_in0, conv_w, conv_b, ln_g, ln_b, qn, kn, bd, w_out0, g1, w_in1, pool_w, pool_scale,
     w_out1, kv_dtype) = params
    zc, q, k, v, sgb = _layer0_head(x, mod0, head_tile, g0, w_in0, qn, kn, bd, conv_w, conv_b, ln_g, ln_b,
                                    kv_dtype)
    za = attention(q, k, v, sgb)
    out = _layers_tail(x, zc, za, mod0, mod1, tail_tile, g1, w_out0, w_in1, pool_w, pool_scale, w_out1)
    return out, k, v


def kernel(x_prompt, x_sample, cache_k_0, cache_v_0, c, c_ctx, norm_g_0, w_ada_0, b_ada_0, w_in_0, conv_w_0, conv_b_0, conv_ln_g_0, conv_ln_b_0, q_norm_0, k_norm_0, rpb_0, w_out_0, norm_g_1, w_ada_1, b_ada_1, w_in_1, pool_w_1, pool_scale_1, w_out_1):
    bp, lp, _ = x_prompt.shape
    bs, ls, _ = x_sample.shape
    past = cache_k_0.shape[1]

    cond = jnp.concatenate([c_ctx[None, :], c, jnp.zeros((ADA_ROWS - 1 - bs, D_MODEL), F32)], axis=0)
    m0, m1 = _ada_modulation(cond, w_ada_0, b_ada_0, w_ada_1, b_ada_1)
    split = lambda m, lo, hi: m[lo:hi].reshape(hi - lo, 3, D_MODEL)
    mod0_p, mod1_p = split(m0, 0, 1), split(m1, 0, 1)
    mod0_s, mod1_s = split(m0, 1, 1 + bs), split(m1, 1, 1 + bs)

    head_id = jnp.arange(D_ATT) // HEAD_DIM
    bd = (head_id[:, None] == head_id[None, :]).astype(BF16)
    common = (norm_g_0.reshape(1, D_MODEL), w_in_0.astype(BF16), conv_w_0, conv_b_0, conv_ln_g_0,
              conv_ln_b_0, jnp.tile(q_norm_0, NA_HEADS).reshape(1, D_ATT),
              jnp.tile(k_norm_0, NA_HEADS).reshape(1, D_ATT), bd, w_out_0.astype(BF16),
              norm_g_1.reshape(1, D_MODEL), w_in_1.astype(BF16), pool_w_1.astype(BF16), pool_scale_1,
              w_out_1.astype(BF16))

    y_prompt, k_ctx, v_ctx = _path(x_prompt, mod0_p, mod1_p, lp, lp, common + (F32,), _context_attention)

    bias = _window_bias(rpb_0)
    kc3 = cache_k_0.reshape(bs, past, D_ATT)
    vc3 = cache_v_0.reshape(bs, past, D_ATT)
    na = lambda q3, k3, v3, sgb3: _neighbourhood_attention(q3, k3, v3, kc3, vc3, bias, sgb3, 32)
    y_sample, _, _ = _path(x_sample, mod0_s, mod1_s, 512, 512, common + (BF16,), na)

    return (y_prompt, y_sample,
            k_ctx.reshape(bp, lp, NA_HEADS, HEAD_DIM), v_ctx.reshape(bp, lp, NA_HEADS, HEAD_DIM))
```

```python
import functools

import jax
import jax.numpy as jnp
from jax import lax
from jax.experimental import pallas as pl
from jax.experimental.pallas import tpu as pltpu

F32 = jnp.float32
BF16 = jnp.bfloat16

D_MODEL = 1024
EPS = 1e-6
D_CONV = 512
CONV_K = 31
CONV_HALO = 16
NA_HEADS = 8
HEAD_DIM = 64
D_ATT = NA_HEADS * HEAD_DIM
HEADS_PER_VREG = 2
N_HEAD_PAIRS = NA_HEADS // HEADS_PER_VREG
PAIR_W = HEADS_PER_VREG * HEAD_DIM
GRID_W = 64
NA_ROWS = 8
NA_COLS = 16
ATT_SCALE = HEAD_DIM ** -0.5
LOG2E = 1.4426950408889634
Q_SCALE = ATT_SCALE * LOG2E
D_IN_EVEN = 3 * D_CONV + 4 * D_ATT
POOL_WINDOWS = (2, 4, 8, 16)
POOL_GD = D_MODEL // len(POOL_WINDOWS)
TAIL_HALO = 16
MASK_BIAS = -1e30
SUBLANES = 8
MXU_WIDTH = 256
ADA_ROWS = SUBLANES
VMEM_LIMIT = 48 * 1024 * 1024


def _sigmoid(x):
    return 1.0 / (1.0 + jnp.exp(-x))


def _silu(x):
    return x * _sigmoid(x)


def _resident(shape, index_map):
    return pl.BlockSpec(shape, index_map, pipeline_mode=pl.Buffered(1))


def _cparams(n_axes):
    return pltpu.CompilerParams(dimension_semantics=("parallel",) * n_axes,
                                vmem_limit_bytes=VMEM_LIMIT)


def _ada_kernel(cond_ref, w0_ref, b0_ref, w1_ref, b1_ref, m0_ref, m1_ref):
    s = _silu(cond_ref[...])
    m0_ref[...] = jnp.dot(s, w0_ref[...], preferred_element_type=F32,
                          precision=lax.Precision.HIGHEST) + b0_ref[...]
    m1_ref[...] = jnp.dot(s, w1_ref[...], preferred_element_type=F32,
                          precision=lax.Precision.HIGHEST) + b1_ref[...]


def _ada_modulation(cond, w_ada_0, b_ada_0, w_ada_1, b_ada_1):
    tn = 512
    n3 = 3 * D_MODEL
    wspec = pl.BlockSpec((D_MODEL, tn), lambda j: (0, j))
    bspec = pl.BlockSpec((1, tn), lambda j: (0, j))
    ospec = pl.BlockSpec((ADA_ROWS, tn), lambda j: (0, j))
    return pl.pallas_call(
        _ada_kernel,
        out_shape=(jax.ShapeDtypeStruct((ADA_ROWS, n3), F32),) * 2,
        grid=(n3 // tn,),
        in_specs=[pl.BlockSpec((ADA_ROWS, D_MODEL), lambda j: (0, 0)), wspec, bspec, wspec, bspec],
        out_specs=(ospec, ospec),
        compiler_params=_cparams(1),
        name="ada_modulation",
    )(cond, w_ada_0, b_ada_0.reshape(1, n3), w_ada_1, b_ada_1.reshape(1, n3))


def _modulated_norm(x, g_row, mod_ref):
    ms = jnp.mean(x * x, axis=-1, keepdims=True)
    y = (x * lax.rsqrt(ms + EPS)) * g_row
    return y * (1.0 + mod_ref[0, 1:2, :]) + mod_ref[0, 0:1, :]


def _head_kernel(*refs, tile, n_tiles, chunk):
    if n_tiles > 1:
        xp_ref, x_ref, xn_ref = refs[:3]
        refs = refs[3:]
        x = jnp.concatenate([xp_ref[0], x_ref[0], xn_ref[0]], axis=0)
    else:
        x_ref = refs[0]
        refs = refs[1:]
        x = x_ref[0]
    (mod_ref, g_ref, w_ref, qn_ref, kn_ref, bd_ref, cw_ref, cb_ref, lg_ref, lb_ref,
     zc_ref, q_ref, k_ref, v_ref, sgb_ref, buf_ref, sh_ref, sga_ref, hb_ref, p_ref) = refs
    i = pl.program_id(1)
    h = CONV_HALO
    hb = _modulated_norm(x, g_ref[...], mod_ref).astype(BF16)

    def proj(t, c):
        return jnp.dot(t, w_ref[:, c * D_CONV:(c + 1) * D_CONV], preferred_element_type=F32)

    def head_rms(t, g_row):
        ms = jnp.dot((t * t).astype(BF16), bd_ref[...], preferred_element_type=F32) * (1.0 / HEAD_DIM)
        return (t * lax.rsqrt(ms + EPS)) * g_row

    u = proj(hb, 0) * _sigmoid(proj(hb, 1))
    if n_tiles > 1:
        row = lax.broadcasted_iota(jnp.int32, (tile + 2 * h, 1), 0)
        outside = ((row < h) & (i == 0)) | ((row >= h + tile) & (i == n_tiles - 1))
        buf_ref[...] = jnp.where(outside, 0.0, u)
        hb = hb[h:h + tile]
    else:
        pad = jnp.zeros((h, D_CONV), F32)
        buf_ref[0:h, :] = pad
        buf_ref[h:h + tile, :] = u
        buf_ref[h + tile:2 * h + tile, :] = pad
    span = tile + 2 * h - SUBLANES
    for s in range(SUBLANES):
        sh_ref[s, 0:span, :] = buf_ref[s:s + span, :]
    sga_ref[...] = _silu(proj(hb, 2))
    hb_ref[...] = hb
    off = h - CONV_K // 2
    n_chunks = tile // chunk
    rest = D_IN_EVEN - 3 * D_CONV
    piece = MXU_WIDTH
    chunks_per_piece = n_chunks * piece // rest

    never = lax.broadcasted_iota(jnp.int32, (chunk, D_CONV), 0) < jnp.minimum(pl.program_id(0), 0)
    for c in range(n_chunks):
        base = c * chunk
        if c % chunks_per_piece == 0:
            col = c // chunks_per_piece * piece
            pp = jnp.dot(hb_ref[...], w_ref[:, 3 * D_CONV + col:3 * D_CONV + col + piece],
                         preferred_element_type=F32)
            p_ref[:, col:col + piece] = pp
        anchor = jnp.concatenate([pp[r * chunk:(r + 1) * chunk] for r in range(D_CONV // piece)], axis=1)
        acc = jnp.where(never, anchor, 0.0).reshape(chunk // SUBLANES, SUBLANES, D_CONV)
        for k in range(CONV_K):
            s, a = (off + k) % SUBLANES, (off + k) // SUBLANES * SUBLANES
            taps = sh_ref[s, base + a:base + a + chunk, :].reshape(chunk // SUBLANES, SUBLANES, D_CONV)
            acc = acc + taps * cw_ref[k][None]
        y = acc.reshape(chunk, D_CONV) + cb_ref[...]
        mu = jnp.mean(y, axis=-1, keepdims=True)
        yc = y - mu
        var = jnp.mean(yc * yc, axis=-1, keepdims=True)
        yn = (yc * lax.rsqrt(var + EPS)) * lg_ref[...] + lb_ref[...]
        z = _silu(yn) * sga_ref[base:base + chunk, :]
        zc_ref[0, base:base + chunk, :] = z.astype(zc_ref.dtype)

    q_ref[0] = (head_rms(p_ref[:, 0:D_ATT], qn_ref[...]) * Q_SCALE).astype(q_ref.dtype)
    k_ref[0] = head_rms(p_ref[:, D_ATT:2 * D_ATT], kn_ref[...]).astype(k_ref.dtype)
    v_ref[0] = p_ref[:, 2 * D_ATT:3 * D_ATT].astype(v_ref.dtype)
    sgb_ref[0] = _silu(p_ref[:, 3 * D_ATT:4 * D_ATT])


def _layer0_head(x3, mod, tile, g0, w_in, qn, kn, bd, conv_w, conv_b, ln_g, ln_b, kv_dtype):
    bsz, L, _ = x3.shape
    n_tiles = L // tile
    n_mod = mod.shape[0]
    const2 = lambda b, i: (0, 0)
    main = lambda width: pl.BlockSpec((1, tile, width), lambda b, i: (b, i, 0))
    if n_tiles == 1:
        x_specs, x_ops = [main(D_MODEL)], [x3]
    else:
        prev, nxt = _halo_specs(tile, CONV_HALO, L, D_MODEL)
        x_specs, x_ops = [prev, main(D_MODEL), nxt], [x3, x3, x3]
    vec = pl.BlockSpec((1, D_CONV), const2)
    out = lambda dtype: jax.ShapeDtypeStruct((bsz, L, D_CONV), dtype)
    ext = tile + 2 * CONV_HALO
    return pl.pallas_call(
        functools.partial(_head_kernel, tile=tile, n_tiles=n_tiles, chunk=32),
        out_shape=(out(BF16),
                   out(BF16),
                   out(kv_dtype),
                   out(kv_dtype),
                   out(F32)),
        grid=(bsz, n_tiles),
        in_specs=x_specs + [
            pl.BlockSpec((1, 3, D_MODEL), lambda b, i: (jnp.minimum(b, n_mod - 1), 0, 0)),
            pl.BlockSpec((1, D_MODEL), const2),
            _resident((D_MODEL, D_IN_EVEN), const2),
            vec, vec,
            pl.BlockSpec((D_ATT, D_ATT), const2),
            pl.BlockSpec((CONV_K, SUBLANES, D_CONV), lambda b, i: (0, 0, 0)),
            vec, vec, vec],
        out_specs=(main(D_CONV),) * 5,
        scratch_shapes=[pltpu.VMEM((ext, D_CONV), F32),
                        pltpu.VMEM((SUBLANES, ext - SUBLANES, D_CONV), F32),
                        pltpu.VMEM((tile, D_CONV), F32),
                        pltpu.VMEM((tile, D_MODEL), BF16),
                        pltpu.VMEM((tile, D_IN_EVEN - 3 * D_CONV), F32)],
        compiler_params=_cparams(2),
        name="layer0_head",
    )(*x_ops, mod, g0, w_in, qn, kn, bd,
      jnp.broadcast_to(conv_w[:, None, :], (CONV_K, SUBLANES, D_CONV)),
      conv_b.reshape(1, D_CONV), ln_g.reshape(1, D_CONV), ln_b.reshape(1, D_CONV))


def _halo_specs(tile, halo, seq, width):
    r = tile // halo
    last = seq // halo - 1
    prev = pl.BlockSpec((1, halo, width), lambda b, i: (b, jnp.maximum(i * r - 1, 0), 0))
    nxt = pl.BlockSpec((1, halo, width), lambda b, i: (b, jnp.minimum((i + 1) * r, last), 0))
    return prev, nxt


def _dot_nt(a, b):
    return lax.dot_general(a, b, (((1,), (1,)), ((), ())), preferred_element_type=F32)


def _head_masks():
    lane = lax.broadcasted_iota(jnp.int32, (1, PAIR_W), 1)
    first = lane < HEAD_DIM
    return (first, jnp.logical_not(first))


def _keep(mask, t):
    return jnp.where(mask, t, jnp.zeros_like(t))


def _ctx_attn_kernel(q_ref, k_ref, v_ref, sgb_ref, z_ref):
    masks = _head_masks()

    def cols(p):
        return slice(p * PAIR_W, (p + 1) * PAIR_W)

    def scores(p, e):
        return _dot_nt(_keep(masks[e], q_ref[0, :, cols(p)]), k_ref[0, :, cols(p)].astype(BF16))

    def finish(p, e, s):
        pr = jnp.exp2(s - jnp.max(s, axis=-1, keepdims=True))
        l = jnp.sum(pr, axis=-1, keepdims=True)
        vb = _keep(masks[e], v_ref[0, :, cols(p)].astype(BF16))
        return jnp.dot(pr.astype(BF16), vb, preferred_element_type=F32) / l

    chains = [(p, e) for p in range(N_HEAD_PAIRS) for e in range(HEADS_PER_VREG)]
    done = {}
    ahead = 2
    pending = [scores(*chains[c]) for c in range(ahead)]
    for c, (p, e) in enumerate(chains):
        if c + ahead < len(chains):
            pending.append(scores(*chains[c + ahead]))
        done[(p, e)] = finish(p, e, pending.pop(0))
    for p in range(N_HEAD_PAIRS):
        o = done[(p, 0)] + done[(p, 1)]
        z_ref[0, :, cols(p)] = (o * sgb_ref[0, :, cols(p)]).astype(z_ref.dtype)


def _context_attention(q3, k3, v3, sgb3):
    bsz, L, _ = q3.shape
    blk = pl.BlockSpec((1, L, D_ATT), lambda b: (b, 0, 0))
    return pl.pallas_call(
        _ctx_attn_kernel,
        out_shape=jax.ShapeDtypeStruct((bsz, L, D_ATT), BF16),
        grid=(bsz,),
        in_specs=[blk, blk, blk, blk],
        out_specs=blk,
        compiler_params=_cparams(1),
        name="context_attention",
    )(q3, k3, v3, sgb3)


def _na_kernel(q_ref, k_ref, v_ref, kc_ref, vc_ref, bias_ref, sgb_ref, z_ref, kcb_ref, vcb_ref,
               *, rows_per_step, n_rows):
    kcb_ref[...] = kc_ref[0].astype(BF16)
    vcb_ref[...] = vc_ref[0].astype(BF16)

    def group(rt, carry):
        _na_row_group(rt, q_ref, k_ref, v_ref, kcb_ref, vcb_ref, bias_ref, sgb_ref, z_ref,
                      rows_per_step=rows_per_step, n_rows=n_rows)
        return carry

    lax.fori_loop(0, n_rows // rows_per_step, group, 0)


def _na_row_group(rt, q_ref, k_ref, v_ref, kcb_ref, vcb_ref, bias_ref, sgb_ref, z_ref,
                  *, rows_per_step, n_rows):
    masks = _head_masks()
    kc = kcb_ref[...]
    vc = vcb_ref[...]
    win = NA_ROWS * GRID_W
    stack = HEADS_PER_VREG * GRID_W

    def q_rows(j):
        return pl.ds(pl.multiple_of((rt * rows_per_step + j) * GRID_W, GRID_W), GRID_W)

    q_stack = jnp.concatenate(
        [_keep(hm, q_ref[0, q_rows(j), :]) for j in range(rows_per_step) for hm in masks], axis=0)
    s_ctx = _dot_nt(q_stack, kc)

    def window(j):
        r = rt * rows_per_step + j
        r_start = jnp.clip(r - NA_ROWS // 2, 0, n_rows - NA_ROWS)
        ws = pl.ds(pl.multiple_of(r_start * GRID_W, GRID_W), win)
        return r - r_start, ws

    def scores(j):
        r_off, ws = window(j)
        return _dot_nt(q_stack[j * stack:(j + 1) * stack], k_ref[0, ws, :]) + bias_ref[r_off, 0]

    def finish(j, s_win):
        _, ws = window(j)
        sc = s_ctx[j * stack:(j + 1) * stack]
        m = jnp.maximum(jnp.max(s_win, axis=-1, keepdims=True), jnp.max(sc, axis=-1, keepdims=True))
        p_win = jnp.exp2(s_win - m)
        p_ctx = jnp.exp2(sc - m)
        l = jnp.sum(p_win, axis=-1, keepdims=True) + jnp.sum(p_ctx, axis=-1, keepdims=True)
        o_win = jnp.dot(p_win.astype(BF16), v_ref[0, ws, :], preferred_element_type=F32)
        return o_win, p_ctx.astype(BF16), l

    done = []
    ahead = 2
    pending = [scores(j) for j in range(min(ahead, rows_per_step))]
    for j in range(rows_per_step):
        if j + ahead < rows_per_step:
            pending.append(scores(j + ahead))
        done.append(finish(j, pending.pop(0)))

    o_win = jnp.concatenate([t[0] for t in done], axis=0)
    p_ctx = jnp.concatenate([t[1] for t in done], axis=0)
    l = jnp.concatenate([t[2] for t in done], axis=0)
    o = (o_win + jnp.dot(p_ctx, vc, preferred_element_type=F32)) / l
    for j in range(rows_per_step):
        oj = jnp.where(masks[0], o[j * stack:j * stack + GRID_W], o[j * stack + GRID_W:(j + 1) * stack])
        z_ref[0, q_rows(j), :] = (oj * sgb_ref[0, q_rows(j), :]).astype(z_ref.dtype)


def _window_bias(rpb):
    qc = jnp.arange(GRID_W)
    kcol = jnp.arange(GRID_W)
    start = jnp.clip(qc - NA_COLS // 2, 0, GRID_W - NA_COLS)
    ok = (kcol[None, :] >= start[:, None]) & (kcol[None, :] < start[:, None] + NA_COLS)
    rel_c = kcol[None, :] - qc[:, None] + NA_COLS - 1
    r_off = jnp.arange(NA_ROWS)
    rel_r = jnp.arange(NA_ROWS)[None, :] - r_off[:, None] + NA_ROWS - 1
    onehot = ((rel_c[None] == jnp.arange(2 * NA_COLS - 1)[:, None, None]) & ok[None]).astype(F32)
    rows = rpb.astype(F32)[:, rel_r]
    b = jnp.einsum('hrjd,dqk->rhqjk', rows, onehot, precision=lax.Precision.HIGHEST)
    b = jnp.where(ok[None, None, :, None, :], b * LOG2E, MASK_BIAS)
    return b.reshape(NA_ROWS, N_HEAD_PAIRS, HEADS_PER_VREG * GRID_W, NA_ROWS * GRID_W)


def _neighbourhood_attention(q3, k3, v3, kc3, vc3, bias, sgb3, rows_per_step):
    bsz, L, _ = q3.shape
    n_rows = L // GRID_W
    past = kc3.shape[1]
    full = pl.BlockSpec((1, L, PAIR_W), lambda b, p: (b, 0, p))
    ctx = pl.BlockSpec((1, past, PAIR_W), lambda b, p: (b, 0, p))
    bias_spec = pl.BlockSpec((NA_ROWS, 1, HEADS_PER_VREG * GRID_W, NA_ROWS * GRID_W),
                             lambda b, p: (0, p, 0, 0))
    return pl.pallas_call(
        functools.partial(_na_kernel, rows_per_step=rows_per_step, n_rows=n_rows),
        out_shape=jax.ShapeDtypeStruct((bsz, L, D_ATT), BF16),
        grid=(bsz, N_HEAD_PAIRS),
        in_specs=[full, full, full, ctx, ctx, bias_spec, full],
        out_specs=full,
        scratch_shapes=[pltpu.VMEM((past, PAIR_W), BF16), pltpu.VMEM((past, PAIR_W), BF16)],
        compiler_params=_cparams(2),
        name="neighbourhood_attention",
    )(q3, k3, v3, kc3, vc3, bias, sgb3)


def _shift_up(x, k):
    n = x.shape[0]
    return pltpu.roll(x, n - k, 0)


def _pool_window_sums(u, w):
    c = u
    m = 1
    while 2 * m < w:
        c = c + _shift_up(c, m)
        m *= 2
    return pltpu.roll(c, w // 2, 0) + c


def _tail_kernel(*refs, tile, n_tiles, seq):
    if n_tiles > 1:
        (xp_ref, x_ref, xn_ref, cp_ref, zc_ref, cn_ref, ap_ref, za_ref, an_ref,
         mod0_ref, mod1_ref, g1_ref, wo0_ref, w1_ref, pw_ref, ps_ref, wo1_ref, o_ref) = refs
        ext = lambda p, m, n: jnp.concatenate([p[0], m[0], n[0]], axis=0)
        x, zc, za = ext(xp_ref, x_ref, xn_ref), ext(cp_ref, zc_ref, cn_ref), ext(ap_ref, za_ref, an_ref)
    else:
        (x_ref, zc_ref, za_ref,
         mod0_ref, mod1_ref, g1_ref, wo0_ref, w1_ref, pw_ref, ps_ref, wo1_ref, o_ref) = refs
        rows = lambda ref: ref[...].reshape(tile, ref.shape[-1])
        x, zc, za = rows(x_ref), rows(zc_ref), rows(za_ref)
    seqs = max(tile // seq, 1)
    i = pl.program_id(1)
    h = TAIL_HALO
    out0 = (jnp.dot(zc, wo0_ref[0:D_CONV, :], preferred_element_type=F32)
            + jnp.dot(za, wo0_ref[D_CONV:D_CONV + D_ATT, :], preferred_element_type=F32))
    y = x + mod0_ref[0, 2:3, :] * out0
    hb = _modulated_norm(y, g1_ref[...], mod1_ref).astype(BF16)
    u = jnp.dot(hb, w1_ref[:, 0:D_MODEL], preferred_element_type=F32)
    if n_tiles > 1:
        row = lax.broadcasted_iota(jnp.int32, (tile + 2 * h, 1), 0)
        outside = ((row < h) & (i == 0)) | ((row >= h + tile) & (i == n_tiles - 1))
        u = jnp.where(outside, 0.0, u)
        y = y[h:h + tile]
        hb = hb[h:h + tile]
        t = (i * tile).astype(F32) + lax.broadcasted_iota(jnp.int32, (tile, 1), 0).astype(F32)
        starts = [h]
    else:
        pad = jnp.zeros((h, D_MODEL), F32)
        parts = [pad]
        for s in range(seqs):
            parts += [u[s * seq:(s + 1) * seq], pad]
        u = jnp.concatenate(parts, axis=0)
        pos = lax.broadcasted_iota(jnp.int32, (seq, 1), 0).astype(F32)
        t = jnp.concatenate([pos] * seqs, axis=0)
        starts = [h + s * (seq + h) for s in range(seqs)]
    span = tile // len(starts)
    main = lambda a: jnp.concatenate([a[r0:r0 + span] for r0 in starts], axis=0)
    sg = _silu(jnp.dot(hb, w1_ref[:, D_MODEL:2 * D_MODEL], preferred_element_type=F32))
    out1 = jnp.zeros((tile, D_MODEL), F32)
    for g, w in enumerate(POOL_WINDOWS):
        cols = slice(g * POOL_GD, (g + 1) * POOL_GD)
        ug = u[:, cols]
        s = main(_pool_window_sums(ug, w))
        cnt = jnp.minimum(t + (w - w // 2), float(seq)) - jnp.maximum(t - w // 2, 0.0)
        d = s / cnt - main(ug)
        yp = jnp.dot(d.astype(BF16), pw_ref[g], preferred_element_type=F32) * ps_ref[:, cols]
        z = (yp * sg[:, cols]).astype(BF16)
        out1 = out1 + jnp.dot(z, wo1_ref[cols, :], preferred_element_type=F32)
    o_ref[...] = (y + mod1_ref[0, 2:3, :] * out1).reshape(o_ref.shape)


def _layers_tail(x3, zc3, za3, mod0, mod1, tile, g1, w_out0, w_in1, pool_w, pool_scale, w_out1):
    bsz, L, _ = x3.shape
    n_tiles = max(L // tile, 1)
    seqs = max(tile // L, 1)
    n_mod = mod0.shape[0]
    assert seqs == 1 or n_mod == 1, "stacked sequences must share one modulation row"
    const2 = lambda b, i: (0, 0)
    modspec = pl.BlockSpec((1, 3, D_MODEL), lambda b, i: (jnp.minimum(b, n_mod - 1), 0, 0))

    def tiled(width):
        main = pl.BlockSpec((seqs, tile // seqs, width), lambda b, i: (b, i, 0))
        if n_tiles == 1:
            return [main]
        prev, nxt = _halo_specs(tile, TAIL_HALO, L, width)
        return [prev, main, nxt]

    def operands(t):
        return [t] if n_tiles == 1 else [t, t, t]

    return pl.pallas_call(
        functools.partial(_tail_kernel, tile=tile, n_tiles=n_tiles, seq=L),
        out_shape=jax.ShapeDtypeStruct((bsz, L, D_MODEL), F32),
        grid=(bsz // seqs, n_tiles),
        in_specs=tiled(D_MODEL) + tiled(D_CONV) + tiled(D_ATT) + [
            modspec, modspec,
            pl.BlockSpec((1, D_MODEL), const2),
            _resident((D_CONV + D_ATT, D_MODEL), const2),
            _resident((D_MODEL, 2 * D_MODEL), const2),
            _resident((len(POOL_WINDOWS), POOL_GD, POOL_GD), lambda b, i: (0, 0, 0)),
            pl.BlockSpec((1, D_MODEL), const2),
            _resident((D_MODEL, D_MODEL), const2)],
        out_specs=pl.BlockSpec((seqs, tile // seqs, D_MODEL), lambda b, i: (b, i, 0)),
        compiler_params=_cparams(2),
        name="layers_tail",
    )(*operands(x3), *operands(zc3), *operands(za3), mod0, mod1, g1, w_out0, w_in1, pool_w,
      pool_scale.reshape(1, D_MODEL), w_out1)


def _path(x, mod0, mod1, head_tile, tail_tile, params, attention):
    (g0, w_in0, conv_w, conv_b, ln_g, ln_b, qn, kn, bd, w_out0, g1, w_in1, pool_w, pool_scale,
     w_out1, kv_dtype) = params
    zc, q, k, v, sgb = _layer0_head(x, mod0, head_tile, g0, w_in0, qn, kn, bd, conv_w, conv_b, ln_g, ln_b,
                                    kv_dtype)
    za = attention(q, k, v, sgb)
    out = _layers_tail(x, zc, za, mod0, mod1, tail_tile, g1, w_out0, w_in1, pool_w, pool_scale, w_out1)
    return out, k, v


def kernel(x_prompt, x_sample, cache_k_0, cache_v_0, c, c_ctx, norm_g_0, w_ada_0, b_ada_0, w_in_0, conv_w_0, conv_b_0, conv_ln_g_0, conv_ln_b_0, q_norm_0, k_norm_0, rpb_0, w_out_0, norm_g_1, w_ada_1, b_ada_1, w_in_1, pool_w_1, pool_scale_1, w_out_1):
    bp, lp, _ = x_prompt.shape
    bs, ls, _ = x_sample.shape
    past = cache_k_0.shape[1]

    cond = jnp.concatenate([c_ctx[None, :], c, jnp.zeros((ADA_ROWS - 1 - bs, D_MODEL), F32)], axis=0)
    m0, m1 = _ada_modulation(cond, w_ada_0, b_ada_0, w_ada_1, b_ada_1)
    split = lambda m, lo, hi: m[lo:hi].reshape(hi - lo, 3, D_MODEL)
    mod0_p, mod1_p = split(m0, 0, 1), split(m1, 0, 1)
    mod0_s, mod1_s = split(m0, 1, 1 + bs), split(m1, 1, 1 + bs)

    head_id = jnp.arange(D_ATT) // HEAD_DIM
    bd = (head_id[:, None] == head_id[None, :]).astype(BF16)
    common = (norm_g_0.reshape(1, D_MODEL), w_in_0.astype(BF16), conv_w_0, conv_b_0, conv_ln_g_0,
              conv_ln_b_0, jnp.tile(q_norm_0, NA_HEADS).reshape(1, D_ATT),
              jnp.tile(k_norm_0, NA_HEADS).reshape(1, D_ATT), bd, w_out_0.astype(BF16),
              norm_g_1.reshape(1, D_MODEL), w_in_1.astype(BF16), pool_w_1.astype(BF16), pool_scale_1,
              w_out_1.astype(BF16))

    y_prompt, k_ctx, v_ctx = _path(x_prompt, mod0_p, mod1_p, lp, 2 * lp, common + (F32,),
                                   _context_attention)

    bias = _window_bias(rpb_0)
    kc3 = cache_k_0.reshape(bs, past, D_ATT)
    vc3 = cache_v_0.reshape(bs, past, D_ATT)
    na = lambda q3, k3, v3, sgb3: _neighbourhood_attention(q3, k3, v3, kc3, vc3, bias, sgb3, 32)
    y_sample, _, _ = _path(x_sample, mod0_s, mod1_s, 512, 512, common + (BF16,), na)

    return (y_prompt, y_sample,
            k_ctx.reshape(bp, lp, NA_HEADS, HEAD_DIM), v_ctx.reshape(bp, lp, NA_HEADS, HEAD_DIM))
```

```python
import functools

import jax
import jax.numpy as jnp
from jax import lax
from jax.experimental import pallas as pl
from jax.experimental.pallas import tpu as pltpu

F32 = jnp.float32
BF16 = jnp.bfloat16

D_MODEL = 1024
EPS = 1e-6
D_CONV = 512
CONV_K = 31
CONV_HALO = 16
NA_HEADS = 8
HEAD_DIM = 64
D_ATT = NA_HEADS * HEAD_DIM
HEADS_PER_VREG = 2
N_HEAD_PAIRS = NA_HEADS // HEADS_PER_VREG
PAIR_W = HEADS_PER_VREG * HEAD_DIM
GRID_W = 64
NA_ROWS = 8
NA_COLS = 16
ATT_SCALE = HEAD_DIM ** -0.5
LOG2E = 1.4426950408889634
Q_SCALE = ATT_SCALE * LOG2E
D_IN_EVEN = 3 * D_CONV + 4 * D_ATT
POOL_WINDOWS = (2, 4, 8, 16)
POOL_GD = D_MODEL // len(POOL_WINDOWS)
TAIL_HALO = 16
MASK_BIAS = -1e30
SUBLANES = 8
MXU_WIDTH = 256
ADA_ROWS = SUBLANES
VMEM_LIMIT = 48 * 1024 * 1024


def _sigmoid(x):
    return 1.0 / (1.0 + jnp.exp(-x))


def _silu(x):
    return x * _sigmoid(x)


def _resident(shape, index_map):
    return pl.BlockSpec(shape, index_map, pipeline_mode=pl.Buffered(1))


def _cparams(n_axes):
    return pltpu.CompilerParams(dimension_semantics=("parallel",) * n_axes,
                                vmem_limit_bytes=VMEM_LIMIT)


def _split_bf16(x):
    hi = x.astype(BF16)
    return hi, (x - hi.astype(F32)).astype(BF16)


def _dot_split(a, b):
    a_hi, a_lo = _split_bf16(a)
    b_hi, b_lo = _split_bf16(b)
    dot = functools.partial(jnp.dot, preferred_element_type=F32)
    return dot(a_hi, b_hi) + (dot(a_hi, b_lo) + dot(a_lo, b_hi))


def _ada_kernel(cond_ref, w0_ref, b0_ref, w1_ref, b1_ref, m0_ref, m1_ref):
    s = _silu(cond_ref[...])
    m0_ref[...] = _dot_split(s, w0_ref[...]) + b0_ref[...]
    m1_ref[...] = _dot_split(s, w1_ref[...]) + b1_ref[...]


def _ada_modulation(cond, w_ada_0, b_ada_0, w_ada_1, b_ada_1):
    tn = 512
    n3 = 3 * D_MODEL
    wspec = pl.BlockSpec((D_MODEL, tn), lambda j: (0, j))
    bspec = pl.BlockSpec((1, tn), lambda j: (0, j))
    ospec = pl.BlockSpec((ADA_ROWS, tn), lambda j: (0, j))
    return pl.pallas_call(
        _ada_kernel,
        out_shape=(jax.ShapeDtypeStruct((ADA_ROWS, n3), F32),) * 2,
        grid=(n3 // tn,),
        in_specs=[pl.BlockSpec((ADA_ROWS, D_MODEL), lambda j: (0, 0)), wspec, bspec, wspec, bspec],
        out_specs=(ospec, ospec),
        compiler_params=_cparams(1),
        name="ada_modulation",
    )(cond, w_ada_0, b_ada_0.reshape(1, n3), w_ada_1, b_ada_1.reshape(1, n3))


def _modulated_norm(x, g_row, mod_ref):
    ms = jnp.mean(x * x, axis=-1, keepdims=True)
    y = (x * lax.rsqrt(ms + EPS)) * g_row
    return y * (1.0 + mod_ref[0, 1:2, :]) + mod_ref[0, 0:1, :]


def _head_kernel(*refs, tile, n_tiles, chunk):
    if n_tiles > 1:
        xp_ref, x_ref, xn_ref = refs[:3]
        refs = refs[3:]
        x = jnp.concatenate([xp_ref[0], x_ref[0], xn_ref[0]], axis=0)
    else:
        x_ref = refs[0]
        refs = refs[1:]
        x = x_ref[0]
    (mod_ref, g_ref, w_ref, qn_ref, kn_ref, bd_ref, cw_ref, cb_ref, lg_ref, lb_ref,
     zc_ref, q_ref, k_ref, v_ref, gb_ref, buf_ref, sh_ref, sga_ref, hb_ref, p_ref) = refs
    i = pl.program_id(1)
    h = CONV_HALO
    hb = _modulated_norm(x, g_ref[...], mod_ref).astype(BF16)

    def proj(t, c):
        return jnp.dot(t, w_ref[:, c * D_CONV:(c + 1) * D_CONV], preferred_element_type=F32)

    def head_rms(t, g_row):
        width = t.shape[1]
        ms = jnp.dot((t * t).astype(BF16), bd_ref[0:width, 0:width], preferred_element_type=F32)
        return (t * lax.rsqrt(ms * (1.0 / HEAD_DIM) + EPS)) * g_row

    u = proj(hb, 0) * _sigmoid(proj(hb, 1))
    if n_tiles > 1:
        row = lax.broadcasted_iota(jnp.int32, (tile + 2 * h, 1), 0)
        outside = ((row < h) & (i == 0)) | ((row >= h + tile) & (i == n_tiles - 1))
        buf_ref[...] = jnp.where(outside, 0.0, u)
        hb = hb[h:h + tile]
    else:
        pad = jnp.zeros((h, D_CONV), F32)
        buf_ref[0:h, :] = pad
        buf_ref[h:h + tile, :] = u
        buf_ref[h + tile:2 * h + tile, :] = pad
    span = tile + 2 * h - SUBLANES
    for s in range(SUBLANES):
        sh_ref[s, 0:span, :] = buf_ref[s:s + span, :]
    sga_ref[...] = _silu(proj(hb, 2))
    hb_ref[...] = hb
    off = h - CONV_K // 2
    n_chunks = tile // chunk
    rest = D_IN_EVEN - 3 * D_CONV
    piece = MXU_WIDTH
    chunks_per_piece = n_chunks * piece // rest

    never = lax.broadcasted_iota(jnp.int32, (chunk, D_CONV), 0) < jnp.minimum(pl.program_id(0), 0)
    for c in range(n_chunks):
        base = c * chunk
        if c % chunks_per_piece == 0:
            col = c // chunks_per_piece * piece
            pp = jnp.dot(hb_ref[...], w_ref[:, 3 * D_CONV + col:3 * D_CONV + col + piece],
                         preferred_element_type=F32)
            p_ref[:, col:col + piece] = pp
        anchor = jnp.concatenate([pp[r * chunk:(r + 1) * chunk] for r in range(D_CONV // piece)], axis=1)
        acc = jnp.where(never, anchor, 0.0).reshape(chunk // SUBLANES, SUBLANES, D_CONV)
        for k in range(CONV_K):
            s, a = (off + k) % SUBLANES, (off + k) // SUBLANES * SUBLANES
            taps = sh_ref[s, base + a:base + a + chunk, :].reshape(chunk // SUBLANES, SUBLANES, D_CONV)
            acc = acc + taps * cw_ref[k][None]
        y = acc.reshape(chunk, D_CONV) + cb_ref[...]
        mu = jnp.mean(y, axis=-1, keepdims=True)
        yc = y - mu
        var = jnp.mean(yc * yc, axis=-1, keepdims=True)
        yn = (yc * lax.rsqrt(var + EPS)) * lg_ref[...] + lb_ref[...]
        z = _silu(yn) * sga_ref[base:base + chunk, :]
        zc_ref[0, base:base + chunk, :] = z.astype(zc_ref.dtype)

    q_ref[0] = (head_rms(p_ref[:, 0:D_ATT], qn_ref[...]) * Q_SCALE).astype(q_ref.dtype)
    k_ref[0] = head_rms(p_ref[:, D_ATT:2 * D_ATT], kn_ref[...]).astype(k_ref.dtype)
    v_ref[0] = p_ref[:, 2 * D_ATT:3 * D_ATT].astype(v_ref.dtype)
    gb_ref[0] = p_ref[:, 3 * D_ATT:4 * D_ATT]


def _layer0_head(x3, mod, tile, g0, w_in, qn, kn, bd, conv_w, conv_b, ln_g, ln_b, kv_dtype):
    bsz, L, _ = x3.shape
    n_tiles = L // tile
    n_mod = mod.shape[0]
    const2 = lambda b, i: (0, 0)
    main = lambda width: pl.BlockSpec((1, tile, width), lambda b, i: (b, i, 0))
    if n_tiles == 1:
        x_specs, x_ops = [main(D_MODEL)], [x3]
    else:
        prev, nxt = _halo_specs(tile, CONV_HALO, L, D_MODEL)
        x_specs, x_ops = [prev, main(D_MODEL), nxt], [x3, x3, x3]
    vec = pl.BlockSpec((1, D_CONV), const2)
    out = lambda dtype: jax.ShapeDtypeStruct((bsz, L, D_CONV), dtype)
    ext = tile + 2 * CONV_HALO
    return pl.pallas_call(
        functools.partial(_head_kernel, tile=tile, n_tiles=n_tiles, chunk=32),
        out_shape=(out(BF16),
                   out(BF16),
                   out(kv_dtype),
                   out(kv_dtype),
                   out(F32)),
        grid=(bsz, n_tiles),
        in_specs=x_specs + [
            pl.BlockSpec((1, 3, D_MODEL), lambda b, i: (jnp.minimum(b, n_mod - 1), 0, 0)),
            pl.BlockSpec((1, D_MODEL), const2),
            _resident((D_MODEL, D_IN_EVEN), const2),
            vec, vec,
            pl.BlockSpec((D_ATT, D_ATT), const2),
            pl.BlockSpec((CONV_K, SUBLANES, D_CONV), lambda b, i: (0, 0, 0)),
            vec, vec, vec],
        out_specs=(main(D_CONV),) * 5,
        scratch_shapes=[pltpu.VMEM((ext, D_CONV), F32),
                        pltpu.VMEM((SUBLANES, ext - SUBLANES, D_CONV), F32),
                        pltpu.VMEM((tile, D_CONV), F32),
                        pltpu.VMEM((tile, D_MODEL), BF16),
                        pltpu.VMEM((tile, D_IN_EVEN - 3 * D_CONV), F32)],
        compiler_params=_cparams(2),
        name="layer0_head",
    )(*x_ops, mod, g0, w_in, qn, kn, bd,
      jnp.broadcast_to(conv_w[:, None, :], (CONV_K, SUBLANES, D_CONV)),
      conv_b.reshape(1, D_CONV), ln_g.reshape(1, D_CONV), ln_b.reshape(1, D_CONV))


def _halo_specs(tile, halo, seq, width):
    r = tile // halo
    last = seq // halo - 1
    prev = pl.BlockSpec((1, halo, width), lambda b, i: (b, jnp.maximum(i * r - 1, 0), 0))
    nxt = pl.BlockSpec((1, halo, width), lambda b, i: (b, jnp.minimum((i + 1) * r, last), 0))
    return prev, nxt


def _dot_nt(a, b):
    return lax.dot_general(a, b, (((1,), (1,)), ((), ())), preferred_element_type=F32)


def _head_masks():
    lane = lax.broadcasted_iota(jnp.int32, (1, PAIR_W), 1)
    first = lane < HEAD_DIM
    return (first, jnp.logical_not(first))


def _keep(mask, t):
    return jnp.where(mask, t, jnp.zeros_like(t))


def _ctx_attn_kernel(q_ref, k_ref, v_ref, z_ref):
    masks = _head_masks()

    def cols(p):
        return slice(p * PAIR_W, (p + 1) * PAIR_W)

    seq = q_ref.shape[1]

    def scores(p):
        q = q_ref[0, :, cols(p)]
        q_stack = jnp.concatenate([_keep(hm, q) for hm in masks], axis=0)
        return _dot_nt(q_stack, k_ref[0, :, cols(p)].astype(BF16))

    def finish(p, s):
        pr = jnp.exp2(s - jnp.max(s, axis=-1, keepdims=True))
        l = jnp.sum(pr, axis=-1, keepdims=True)
        o = jnp.dot(pr.astype(BF16), v_ref[0, :, cols(p)].astype(BF16), preferred_element_type=F32) / l
        o = jnp.where(masks[0], o[0:seq], o[seq:2 * seq])
        z_ref[0, :, cols(p)] = o

    ahead = 2
    pending = [scores(p) for p in range(ahead)]
    for p in range(N_HEAD_PAIRS):
        if p + ahead < N_HEAD_PAIRS:
            pending.append(scores(p + ahead))
        finish(p, pending.pop(0))


def _context_attention(q3, k3, v3):
    bsz, L, _ = q3.shape
    blk = pl.BlockSpec((1, L, D_ATT), lambda b: (b, 0, 0))
    return pl.pallas_call(
        _ctx_attn_kernel,
        out_shape=jax.ShapeDtypeStruct((bsz, L, D_ATT), F32),
        grid=(bsz,),
        in_specs=[blk, blk, blk],
        out_specs=blk,
        compiler_params=_cparams(1),
        name="context_attention",
    )(q3, k3, v3)


def _na_kernel(q_ref, k_ref, v_ref, kc_ref, vc_ref, bias_ref, z_ref, kcb_ref, vcb_ref,
               *, rows_per_step, n_rows):
    kcb_ref[...] = kc_ref[0].astype(BF16)
    vcb_ref[...] = vc_ref[0].astype(BF16)

    def group(rt, carry):
        _na_row_group(rt, q_ref, k_ref, v_ref, kcb_ref, vcb_ref, bias_ref, z_ref,
                      rows_per_step=rows_per_step, n_rows=n_rows)
        return carry

    lax.fori_loop(0, n_rows // rows_per_step, group, 0)


def _na_row_group(rt, q_ref, k_ref, v_ref, kcb_ref, vcb_ref, bias_ref, z_ref,
                  *, rows_per_step, n_rows):
    masks = _head_masks()
    kc = kcb_ref[...]
    vc = vcb_ref[...]
    win = NA_ROWS * GRID_W
    stack = HEADS_PER_VREG * GRID_W

    def q_rows(j):
        return pl.ds(pl.multiple_of((rt * rows_per_step + j) * GRID_W, GRID_W), GRID_W)

    q_stack = jnp.concatenate(
        [_keep(hm, q_ref[0, q_rows(j), :]) for j in range(rows_per_step) for hm in masks], axis=0)
    s_ctx = _dot_nt(q_stack, kc)

    def window(j):
        r = rt * rows_per_step + j
        r_start = jnp.clip(r - NA_ROWS // 2, 0, n_rows - NA_ROWS)
        ws = pl.ds(pl.multiple_of(r_start * GRID_W, GRID_W), win)
        return r - r_start, ws

    def scores(j):
        r_off, ws = window(j)
        return _dot_nt(q_stack[j * stack:(j + 1) * stack], k_ref[0, ws, :]) + bias_ref[r_off, 0]

    def finish(j, s_win):
        _, ws = window(j)
        sc = s_ctx[j * stack:(j + 1) * stack]
        m = jnp.maximum(jnp.max(s_win, axis=-1, keepdims=True), jnp.max(sc, axis=-1, keepdims=True))
        p_win = jnp.exp2(s_win - m)
        p_ctx = jnp.exp2(sc - m)
        l = jnp.sum(p_win, axis=-1, keepdims=True) + jnp.sum(p_ctx, axis=-1, keepdims=True)
        o_win = jnp.dot(p_win.astype(BF16), v_ref[0, ws, :], preferred_element_type=F32)
        return o_win, p_ctx.astype(BF16), l

    done = []
    ahead = 2
    pending = [scores(j) for j in range(min(ahead, rows_per_step))]
    for j in range(rows_per_step):
        if j + ahead < rows_per_step:
            pending.append(scores(j + ahead))
        done.append(finish(j, pending.pop(0)))

    o_win = jnp.concatenate([t[0] for t in done], axis=0)
    p_ctx = jnp.concatenate([t[1] for t in done], axis=0)
    l = jnp.concatenate([t[2] for t in done], axis=0)
    o = (o_win + jnp.dot(p_ctx, vc, preferred_element_type=F32)) / l
    for j in range(rows_per_step):
        oj = jnp.where(masks[0], o[j * stack:j * stack + GRID_W], o[j * stack + GRID_W:(j + 1) * stack])
        z_ref[0, q_rows(j), :] = oj


def _window_bias(rpb):
    qc = jnp.arange(GRID_W)
    kcol = jnp.arange(GRID_W)
    start = jnp.clip(qc - NA_COLS // 2, 0, GRID_W - NA_COLS)
    ok = (kcol[None, :] >= start[:, None]) & (kcol[None, :] < start[:, None] + NA_COLS)
    rel_c = kcol[None, :] - qc[:, None] + NA_COLS - 1
    r_off = jnp.arange(NA_ROWS)
    rel_r = jnp.arange(NA_ROWS)[None, :] - r_off[:, None] + NA_ROWS - 1
    onehot = ((rel_c[None] == jnp.arange(2 * NA_COLS - 1)[:, None, None]) & ok[None]).astype(F32)
    rows = rpb.astype(F32)[:, rel_r]
    b = jnp.einsum('hrjd,dqk->rhqjk', rows, onehot, precision=lax.Precision.HIGHEST)
    b = jnp.where(ok[None, None, :, None, :], b * LOG2E, MASK_BIAS)
    return b.reshape(NA_ROWS, N_HEAD_PAIRS, HEADS_PER_VREG * GRID_W, NA_ROWS * GRID_W)


def _neighbourhood_attention(q3, k3, v3, kc3, vc3, bias, rows_per_step):
    bsz, L, _ = q3.shape
    n_rows = L // GRID_W
    past = kc3.shape[1]
    full = pl.BlockSpec((1, L, PAIR_W), lambda b, p: (b, 0, p))
    ctx = pl.BlockSpec((1, past, PAIR_W), lambda b, p: (b, 0, p))
    bias_spec = pl.BlockSpec((NA_ROWS, 1, HEADS_PER_VREG * GRID_W, NA_ROWS * GRID_W),
                             lambda b, p: (0, p, 0, 0))
    return pl.pallas_call(
        functools.partial(_na_kernel, rows_per_step=rows_per_step, n_rows=n_rows),
        out_shape=jax.ShapeDtypeStruct((bsz, L, D_ATT), F32),
        grid=(bsz, N_HEAD_PAIRS),
        in_specs=[full, full, full, ctx, ctx, bias_spec],
        out_specs=full,
        scratch_shapes=[pltpu.VMEM((past, PAIR_W), BF16), pltpu.VMEM((past, PAIR_W), BF16)],
        compiler_params=_cparams(2),
        name="neighbourhood_attention",
    )(q3, k3, v3, kc3, vc3, bias)


def _shift_up(x, k):
    n = x.shape[0]
    return pltpu.roll(x, n - k, 0)


def _pool_window_sums(u, w):
    c = u
    m = 1
    while 2 * m < w:
        c = c + _shift_up(c, m)
        m *= 2
    return pltpu.roll(c, w // 2, 0) + c


def _tail_kernel(*refs, tile, n_tiles, seq):
    if n_tiles > 1:
        streams = [jnp.concatenate([refs[3 * n][0], refs[3 * n + 1][0], refs[3 * n + 2][0]], axis=0)
                   for n in range(4)]
        refs = refs[12:]
    else:
        streams = [ref[...].reshape(tile, ref.shape[-1]) for ref in refs[:4]]
        refs = refs[4:]
    x, zc, att, gb = streams
    mod0_ref, mod1_ref, g1_ref, wo0_ref, w1_ref, pw_ref, ps_ref, wo1_ref, o_ref = refs
    seqs = max(tile // seq, 1)
    i = pl.program_id(1)
    h = TAIL_HALO
    za = (att * _silu(gb)).astype(BF16)
    out0 = (jnp.dot(zc, wo0_ref[0:D_CONV, :], preferred_element_type=F32)
            + jnp.dot(za, wo0_ref[D_CONV:D_CONV + D_ATT, :], preferred_element_type=F32))
    y = x + mod0_ref[0, 2:3, :] * out0
    hb = _modulated_norm(y, g1_ref[...], mod1_ref).astype(BF16)
    u = jnp.dot(hb, w1_ref[:, 0:D_MODEL], preferred_element_type=F32)
    if n_tiles > 1:
        row = lax.broadcasted_iota(jnp.int32, (tile + 2 * h, 1), 0)
        outside = ((row < h) & (i == 0)) | ((row >= h + tile) & (i == n_tiles - 1))
        u = jnp.where(outside, 0.0, u)
        y = y[h:h + tile]
        hb = hb[h:h + tile]
        t = (i * tile).astype(F32) + lax.broadcasted_iota(jnp.int32, (tile, 1), 0).astype(F32)
        starts = [h]
    else:
        pad = jnp.zeros((h, D_MODEL), F32)
        parts = [pad]
        for s in range(seqs):
            parts += [u[s * seq:(s + 1) * seq], pad]
        u = jnp.concatenate(parts, axis=0)
        pos = lax.broadcasted_iota(jnp.int32, (seq, 1), 0).astype(F32)
        t = jnp.concatenate([pos] * seqs, axis=0)
        starts = [h + s * (seq + h) for s in range(seqs)]
    span = tile // len(starts)
    main = lambda a: jnp.concatenate([a[r0:r0 + span] for r0 in starts], axis=0)
    sg = _silu(jnp.dot(hb, w1_ref[:, D_MODEL:2 * D_MODEL], preferred_element_type=F32))
    out1 = jnp.zeros((tile, D_MODEL), F32)
    for g, w in enumerate(POOL_WINDOWS):
        cols = slice(g * POOL_GD, (g + 1) * POOL_GD)
        ug = u[:, cols]
        s = main(_pool_window_sums(ug, w))
        cnt = jnp.minimum(t + (w - w // 2), float(seq)) - jnp.maximum(t - w // 2, 0.0)
        d = s / cnt - main(ug)
        yp = jnp.dot(d.astype(BF16), pw_ref[g], preferred_element_type=F32) * ps_ref[:, cols]
        z = (yp * sg[:, cols]).astype(BF16)
        out1 = out1 + jnp.dot(z, wo1_ref[cols, :], preferred_element_type=F32)
    o_ref[...] = (y + mod1_ref[0, 2:3, :] * out1).reshape(o_ref.shape)


def _layers_tail(x3, zc3, att3, gb3, mod0, mod1, tile, g1, w_out0, w_in1, pool_w, pool_scale, w_out1):
    bsz, L, _ = x3.shape
    n_tiles = max(L // tile, 1)
    seqs = max(tile // L, 1)
    n_mod = mod0.shape[0]
    assert seqs == 1 or n_mod == 1, "stacked sequences must share one modulation row"
    const2 = lambda b, i: (0, 0)
    modspec = pl.BlockSpec((1, 3, D_MODEL), lambda b, i: (jnp.minimum(b, n_mod - 1), 0, 0))

    def tiled(width):
        main = pl.BlockSpec((seqs, tile // seqs, width), lambda b, i: (b, i, 0))
        if n_tiles == 1:
            return [main]
        prev, nxt = _halo_specs(tile, TAIL_HALO, L, width)
        return [prev, main, nxt]

    def operands(t):
        return [t] if n_tiles == 1 else [t, t, t]

    return pl.pallas_call(
        functools.partial(_tail_kernel, tile=tile, n_tiles=n_tiles, seq=L),
        out_shape=jax.ShapeDtypeStruct((bsz, L, D_MODEL), F32),
        grid=(bsz // seqs, n_tiles),
        in_specs=tiled(D_MODEL) + tiled(D_CONV) + tiled(D_ATT) + tiled(D_ATT) + [
            modspec, modspec,
            pl.BlockSpec((1, D_MODEL), const2),
            _resident((D_CONV + D_ATT, D_MODEL), const2),
            _resident((D_MODEL, 2 * D_MODEL), const2),
            _resident((len(POOL_WINDOWS), POOL_GD, POOL_GD), lambda b, i: (0, 0, 0)),
            pl.BlockSpec((1, D_MODEL), const2),
            _resident((D_MODEL, D_MODEL), const2)],
        out_specs=pl.BlockSpec((seqs, tile // seqs, D_MODEL), lambda b, i: (b, i, 0)),
        compiler_params=_cparams(2),
        name="layers_tail",
    )(*operands(x3), *operands(zc3), *operands(att3), *operands(gb3), mod0, mod1, g1, w_out0, w_in1,
      pool_w, pool_scale.reshape(1, D_MODEL), w_out1)


def _path(x, mod0, mod1, head_tile, tail_tile, params, attention):
    (g0, w_in0, conv_w, conv_b, ln_g, ln_b, qn, kn, bd, w_out0, g1, w_in1, pool_w, pool_scale,
     w_out1, kv_dtype) = params
    zc, q, k, v, gb = _layer0_head(x, mod0, head_tile, g0, w_in0, qn, kn, bd, conv_w, conv_b, ln_g, ln_b,
                                   kv_dtype)
    att = attention(q, k, v)
    out = _layers_tail(x, zc, att, gb, mod0, mod1, tail_tile, g1, w_out0, w_in1, pool_w, pool_scale,
                       w_out1)
    return out, k, v


def kernel(x_prompt, x_sample, cache_k_0, cache_v_0, c, c_ctx, norm_g_0, w_ada_0, b_ada_0, w_in_0, conv_w_0, conv_b_0, conv_ln_g_0, conv_ln_b_0, q_norm_0, k_norm_0, rpb_0, w_out_0, norm_g_1, w_ada_1, b_ada_1, w_in_1, pool_w_1, pool_scale_1, w_out_1):
    bp, lp, _ = x_prompt.shape
    bs, ls, _ = x_sample.shape
    past = cache_k_0.shape[1]

    cond = jnp.concatenate([c_ctx[None, :], c, jnp.zeros((ADA_ROWS - 1 - bs, D_MODEL), F32)], axis=0)
    m0, m1 = _ada_modulation(cond, w_ada_0, b_ada_0, w_ada_1, b_ada_1)
    split = lambda m, lo, hi: m[lo:hi].reshape(hi - lo, 3, D_MODEL)
    mod0_p, mod1_p = split(m0, 0, 1), split(m1, 0, 1)
    mod0_s, mod1_s = split(m0, 1, 1 + bs), split(m1, 1, 1 + bs)

    head_id = jnp.arange(D_ATT) // HEAD_DIM
    bd = (head_id[:, None] == head_id[None, :]).astype(BF16)
    common = (norm_g_0.reshape(1, D_MODEL), w_in_0.astype(BF16), conv_w_0, conv_b_0, conv_ln_g_0,
              conv_ln_b_0, jnp.tile(q_norm_0, NA_HEADS).reshape(1, D_ATT),
              jnp.tile(k_norm_0, NA_HEADS).reshape(1, D_ATT), bd, w_out_0.astype(BF16),
              norm_g_1.reshape(1, D_MODEL), w_in_1.astype(BF16), pool_w_1.astype(BF16), pool_scale_1,
              w_out_1.astype(BF16))

    y_prompt, k_ctx, v_ctx = _path(x_prompt, mod0_p, mod1_p, lp, 2 * lp, common + (F32,),
                                   _context_attention)

    bias = _window_bias(rpb_0)
    kc3 = cache_k_0.reshape(bs, past, D_ATT)
    vc3 = cache_v_0.reshape(bs, past, D_ATT)
    na = lambda q3, k3, v3: _neighbourhood_attention(q3, k3, v3, kc3, vc3, bias, 32)
    y_sample, _, _ = _path(x_sample, mod0_s, mod1_s, 512, 512, common + (BF16,), na)

    return (y_prompt, y_sample,
            k_ctx.reshape(bp, lp, NA_HEADS, HEAD_DIM), v_ctx.reshape(bp, lp, NA_HEADS, HEAD_DIM))
```
